```python
import jax
import jax.numpy as jnp
from jax import lax
import numpy as np

D_MODEL = 4096
BATCH = 2
SEQ = 4096
DEPTH = 2

GRID_W = 64
CTX_LEN = 256
HEAD_DIM = 128
NA_HEADS = 12
RET_HEADS = 12
SGU_GROUPS = 8
SGU_GROUP_DIM = 128
SGU_CHUNK = 128
RET_CHUNK = 128
WIN_ROWS = 8
WIN_COLS = 16
NA_WIDTH = NA_HEADS * HEAD_DIM
RET_WIDTH = RET_HEADS * HEAD_DIM
SGU_WIDTH = SGU_GROUPS * SGU_GROUP_DIM
MIX_WIDTH = NA_WIDTH + RET_WIDTH + SGU_WIDTH
PROJ_NAMES = ('qA', 'kA', 'vA', 'qB', 'kB', 'vB', 'gB', 'uC', 'vC')
PROJ_WIDTHS = (NA_WIDTH, NA_WIDTH, NA_WIDTH, RET_WIDTH, RET_WIDTH, RET_WIDTH, RET_WIDTH, SGU_WIDTH, SGU_WIDTH)
PROJ_OFFSETS = (0, NA_WIDTH, 2 * NA_WIDTH, 3 * NA_WIDTH,
                3 * NA_WIDTH + RET_WIDTH, 3 * NA_WIDTH + 2 * RET_WIDTH, 3 * NA_WIDTH + 3 * RET_WIDTH,
                3 * NA_WIDTH + 4 * RET_WIDTH, 3 * NA_WIDTH + 4 * RET_WIDTH + SGU_WIDTH,
                3 * NA_WIDTH + 4 * RET_WIDTH + 2 * SGU_WIDTH)
IN_WIDTH = PROJ_OFFSETS[-1]
N_EXPERTS = 16
EXPERT_FF = 1024
EC_CAPACITY = 2
N_MOD = 6
ROPE_BASE = 10000.0
NORM_EPS = 1e-6

kernel_name = 'hybrid_na_retention_sgu_ecmoe_dit'


def rmsnorm(x, g):
    xf = x.astype(jnp.float32)
    xf = xf * lax.rsqrt(jnp.mean(xf * xf, axis=-1, keepdims=True) + NORM_EPS)
    return (xf * g.astype(jnp.float32)).astype(x.dtype)


def modulate(x, shift, scale):
    return x * (1.0 + scale) + shift


def to_heads(t):
    b, l, w = t.shape
    return t.reshape(b, l, w // HEAD_DIM, HEAD_DIM).transpose(0, 2, 1, 3)


def from_heads(t):
    b, h, l, d = t.shape
    return t.transpose(0, 2, 1, 3).reshape(b, l, h * d)


def flip_seq(t):
    return jnp.flip(t, axis=2)


def rope_1d(x, pos):
    half = x.shape[-1] // 2
    inv_freq = ROPE_BASE ** (-jnp.arange(half, dtype=jnp.float32) / half)
    ang = pos.astype(jnp.float32)[:, None] * inv_freq[None, :]
    cos, sin = jnp.cos(ang), jnp.sin(ang)
    x1 = x[..., :half].astype(jnp.float32)
    x2 = x[..., half:].astype(jnp.float32)
    return jnp.concatenate([x1 * cos - x2 * sin, x1 * sin + x2 * cos], axis=-1).astype(x.dtype)


def axial_rope(x, row_pos, col_pos):
    h = x.shape[-1] // 2
    return jnp.concatenate([rope_1d(x[..., :h], row_pos), rope_1d(x[..., h:], col_pos)], axis=-1)


def neighbourhood_attention(q, k, v, k_ctx, v_ctx, rpb):
    b, h, l, d = q.shape
    rows = l // GRID_W
    win_rows = min(WIN_ROWS, rows)
    n_win = win_rows * WIN_COLS
    scale = d ** -0.5
    q_grid = q.reshape(b, h, rows, GRID_W, d)
    k_grid = k.reshape(b, h, rows, GRID_W, d)
    v_grid = v.reshape(b, h, rows, GRID_W, d)
    col_start = np.clip(np.arange(GRID_W) - WIN_COLS // 2, 0, GRID_W - WIN_COLS)
    col_idx = col_start[:, None] + np.arange(WIN_COLS)[None, :]
    col_bias_idx = col_idx - np.arange(GRID_W)[:, None] + (WIN_COLS - 1)
    rpb_f = rpb.astype(jnp.float32)

    def row_block(r):
        r0 = jnp.clip(r - win_rows // 2, 0, rows - win_rows)
        q_r = lax.dynamic_index_in_dim(q_grid, r, axis=2, keepdims=False)
        k_r = lax.dynamic_slice_in_dim(k_grid, r0, win_rows, axis=2)[:, :, :, col_idx]
        v_r = lax.dynamic_slice_in_dim(v_grid, r0, win_rows, axis=2)[:, :, :, col_idx]
        row_bias_idx = r0 + jnp.arange(win_rows) - r + (WIN_ROWS - 1)
        bias = rpb_f[:, row_bias_idx][:, :, col_bias_idx].transpose(0, 2, 1, 3)
        s_win = jnp.einsum('bhqd,bhrqcd->bhqrc', q_r, k_r).astype(jnp.float32) * scale + bias[None]
        s_ctx = jnp.einsum('bhqd,bhkd->bhqk', q_r, k_ctx).astype(jnp.float32) * scale
        s = jnp.concatenate([s_win.reshape(b, h, GRID_W, n_win), s_ctx], axis=-1)
        p = jax.nn.softmax(s, axis=-1).astype(v.dtype)
        p_win = p[..., :n_win].reshape(b, h, GRID_W, win_rows, WIN_COLS)
        return (jnp.einsum('bhqrc,bhrqcd->bhqd', p_win, v_r)
                + jnp.einsum('bhqk,bhkd->bhqd', p[..., n_win:], v_ctx))

    out = lax.map(row_block, jnp.arange(rows))
    return out.transpose(1, 2, 0, 3, 4).reshape(b, h, l, d)


def context_attention(q, k, v):
    s = jnp.einsum('bhqd,bhkd->bhqk', q, k).astype(jnp.float32) * (q.shape[-1] ** -0.5)
    p = jax.nn.softmax(s, axis=-1).astype(v.dtype)
    return jnp.einsum('bhqk,bhkd->bhqd', p, v)


def retention_chunks(q, k, v, log_decay, state0):
    b, h, l, dk = q.shape
    n_chunks = l // RET_CHUNK
    ld = log_decay.astype(jnp.float32)
    idx = jnp.arange(RET_CHUNK, dtype=jnp.float32)
    diff = idx[:, None] - idx[None, :]
    inner_decay = jnp.where(diff >= 0, jnp.exp(jnp.maximum(diff, 0.0)[None] * ld[:, None, None]), 0.0)
    q_decay = jnp.exp((idx + 1.0)[None, :] * ld[:, None])[..., None]
    k_decay = jnp.exp((RET_CHUNK - 1.0 - idx)[None, :] * ld[:, None])[..., None]
    chunk_decay = jnp.exp(RET_CHUNK * ld)[:, None, None]

    def to_chunks(t):
        return t.astype(jnp.float32).reshape(b, h, n_chunks, RET_CHUNK, -1).transpose(2, 0, 1, 3, 4)

    def step(state, qkv):
        qc, kc, vc = qkv
        inner = jnp.einsum('bhnm,bhme->bhne', jnp.einsum('bhnd,bhmd->bhnm', qc, kc) * inner_decay, vc)
        cross = jnp.einsum('bhnd,bhde->bhne', qc, state) * q_decay
        state = state * chunk_decay + jnp.einsum('bhmd,bhme->bhde', kc * k_decay, vc)
        return state, inner + cross

    state, out = lax.scan(step, state0.astype(jnp.float32), (to_chunks(q), to_chunks(k), to_chunks(v)))
    return out.transpose(1, 2, 0, 3, 4).reshape(b, h, l, -1), state


def retention_state(k, v, log_decay):
    l = k.shape[2]
    ld = log_decay.astype(jnp.float32)
    w = jnp.exp((l - 1.0 - jnp.arange(l, dtype=jnp.float32))[None, :] * ld[:, None])
    return jnp.einsum('bhld,bhle->bhde', k.astype(jnp.float32) * w[..., None], v.astype(jnp.float32))


def retention_output(o, g, norm_g):
    b, h, l, d = o.shape
    of = o.astype(jnp.float32).transpose(0, 2, 1, 3)
    of = of * lax.rsqrt(jnp.mean(of * of, axis=-1, keepdims=True) + NORM_EPS)
    of = of * norm_g.astype(jnp.float32).reshape(h, d)
    return (of.reshape(b, l, h * d) * jax.nn.silu(g.astype(jnp.float32))).astype(g.dtype)


def chunk_spatial_gating(u, v, ln_g, ln_b, w_s, b_s):
    b, l, _ = u.shape
    n_chunks = l // SGU_CHUNK
    u = jax.nn.gelu(u)
    vf = jax.nn.gelu(v).astype(jnp.float32).reshape(b, n_chunks, SGU_CHUNK, SGU_GROUPS, SGU_GROUP_DIM)
    mu = jnp.mean(vf, axis=-1, keepdims=True)
    var = jnp.mean(jnp.square(vf - mu), axis=-1, keepdims=True)
    vn = (vf - mu) * lax.rsqrt(var + NORM_EPS)
    vn = vn * ln_g.astype(jnp.float32).reshape(SGU_GROUPS, SGU_GROUP_DIM) + ln_b.astype(jnp.float32).reshape(SGU_GROUPS, SGU_GROUP_DIM)
    s = jnp.einsum('gpq,bnqgd->bnpgd', w_s.astype(jnp.float32), vn) + b_s.astype(jnp.float32).T[:, :, None]
    return (u.astype(jnp.float32) * s.reshape(b, l, SGU_WIDTH)).astype(u.dtype)


def expert_choice_ffn(x, router_w, w_gate, w_up, w_down):
    b, n, _ = x.shape
    cap = EC_CAPACITY * n // N_EXPERTS
    aff = jax.nn.softmax(x.astype(jnp.float32) @ router_w.astype(jnp.float32), axis=-1)
    gates, idx = lax.top_k(aff.transpose(0, 2, 1), cap)
    b_idx = jnp.arange(b)[:, None, None]
    xin = x[b_idx, idx]
    hid = jax.nn.silu(jnp.einsum('becd,edf->becf', xin, w_gate)) * jnp.einsum('becd,edf->becf', xin, w_up)
    y = jnp.einsum('becf,efd->becd', hid, w_down) * gates[..., None].astype(x.dtype)
    return jnp.zeros_like(x).at[b_idx, idx].add(y)


def project(a, w_in, first, last):
    p = a @ w_in[:, PROJ_OFFSETS[first]:PROJ_OFFSETS[last]]
    base = PROJ_OFFSETS[first]
    return {PROJ_NAMES[i]: p[..., PROJ_OFFSETS[i] - base:PROJ_OFFSETS[i + 1] - base] for i in range(first, last)}


def token_mixers(a_lat, a_ctx, lp, update_ctx):
    b, l, _ = a_lat.shape
    t = jnp.arange(l)
    row_pos, col_pos = t // GRID_W, t % GRID_W
    w_in = lp['w_in']
    lat = project(a_lat, w_in, 0, len(PROJ_NAMES))
    if update_ctx:
        ctx = project(a_ctx, w_in, 0, len(PROJ_NAMES))
    else:
        ctx = {**project(a_ctx, w_in, 1, 3), **project(a_ctx, w_in, 4, 6)}
    k_scale = HEAD_DIM ** -0.5

    kA_c, vA_c = to_heads(ctx['kA']), to_heads(ctx['vA'])
    oA_lat = neighbourhood_attention(to_heads(lat['qA']), to_heads(lat['kA']), to_heads(lat['vA']), kA_c, vA_c, lp['na_rpb'])

    qB = axial_rope(to_heads(lat['qB']), row_pos, col_pos)
    kB = axial_rope(to_heads(lat['kB']), row_pos, col_pos) * k_scale
    vB = to_heads(lat['vB'])
    kB_c, vB_c = to_heads(ctx['kB']) * k_scale, to_heads(ctx['vB'])
    ld_f, ld_b = lp['ret_log_decay'][0], lp['ret_log_decay'][1]
    if update_ctx:
        qB_c = to_heads(ctx['qB'])
        zero = jnp.zeros((b, RET_HEADS, HEAD_DIM, HEAD_DIM), jnp.float32)
        oc_f, st_f = retention_chunks(qB_c, kB_c, vB_c, ld_f, zero)
        oc_b, st_b = retention_chunks(flip_seq(qB_c), flip_seq(kB_c), flip_seq(vB_c), ld_b, zero)
        oB_ctx = oc_f + flip_seq(oc_b)
    else:
        st_f = retention_state(kB_c, vB_c, ld_f)
        st_b = retention_state(flip_seq(kB_c), flip_seq(vB_c), ld_b)
    ol_f, _ = retention_chunks(qB, kB, vB, ld_f, st_f)
    ol_b, _ = retention_chunks(flip_seq(qB), flip_seq(kB), flip_seq(vB), ld_b, st_b)
    oB_lat = retention_output(ol_f + flip_seq(ol_b), lat['gB'], lp['ret_norm_g'])

    sgu = (lp['sgu_ln_g'], lp['sgu_ln_b'], lp['sgu_w'], lp['sgu_b'])
    oC_lat = chunk_spatial_gating(lat['uC'], lat['vC'], *sgu)

    y_lat = jnp.concatenate([from_heads(oA_lat), oB_lat, oC_lat], axis=-1) @ lp['w_out']
    if update_ctx:
        oA_ctx = context_attention(to_heads(ctx['qA']), kA_c, vA_c)
        oB_ctx = retention_output(oB_ctx, ctx['gB'], lp['ret_norm_g'])
        oC_ctx = chunk_spatial_gating(ctx['uC'], ctx['vC'], *sgu)
        y_ctx = jnp.concatenate([from_heads(oA_ctx), oB_ctx, oC_ctx], axis=-1) @ lp['w_out']
    else:
        y_ctx = None
    return y_lat, y_ctx


def hybrid_layer(h_ctx, h_lat, c, c_ctx, lp, update_ctx):
    mod_lat = jnp.split((jax.nn.silu(c) @ lp['ada_w'] + lp['ada_b'])[:, None, :], N_MOD, axis=-1)
    mod_ctx = jnp.split(jax.nn.silu(c_ctx) @ lp['ada_w'] + lp['ada_b'], N_MOD, axis=-1)
    a_lat = modulate(rmsnorm(h_lat, lp['norm1_g']), mod_lat[0], mod_lat[1])
    a_ctx = modulate(rmsnorm(h_ctx, lp['norm1_g']), mod_ctx[0], mod_ctx[1])
    y_lat, y_ctx = token_mixers(a_lat, a_ctx, lp, update_ctx)
    moe = (lp['router_w'], lp['w_gate'], lp['w_up'], lp['w_down'])
    h_lat = h_lat + mod_lat[2] * y_lat
    f_lat = modulate(rmsnorm(h_lat, lp['norm2_g']), mod_lat[3], mod_lat[4])
    h_lat = h_lat + mod_lat[5] * expert_choice_ffn(f_lat, *moe)
    if update_ctx:
        h_ctx = h_ctx + mod_ctx[2] * y_ctx
        f_ctx = modulate(rmsnorm(h_ctx, lp['norm2_g']), mod_ctx[3], mod_ctx[4])
        h_ctx = h_ctx + mod_ctx[5] * expert_choice_ffn(f_ctx, *moe)
    return h_ctx, h_lat


def setup_inputs(seed: int = 0) -> dict:
    key = jax.random.key(seed)
    ks = jax.random.split(key, 22)
    f32 = jnp.float32

    def nrm(k, shape, scale):
        return jax.random.normal(k, shape, f32) * scale

    base_decay = jnp.log(1.0 - 2.0 ** (-5.0 - jnp.arange(RET_HEADS, dtype=f32)))
    ret_log_decay = base_decay * jnp.exp(0.1 * jax.random.normal(ks[11], (DEPTH, 2, RET_HEADS), f32))
    return {
        'x': nrm(ks[0], (BATCH, SEQ, D_MODEL), 1.0),
        'c': nrm(ks[1], (BATCH, D_MODEL), 1.0),
        'ctx': nrm(ks[2], (BATCH, CTX_LEN, D_MODEL), 1.0),
        'c_ctx': nrm(ks[3], (D_MODEL,), 1.0),
        'ada_w': nrm(ks[4], (DEPTH, D_MODEL, N_MOD * D_MODEL), 0.5 * D_MODEL ** -0.5),
        'ada_b': nrm(ks[5], (DEPTH, N_MOD * D_MODEL), 0.01),
        'norm1_g': 1.0 + nrm(ks[6], (DEPTH, D_MODEL), 0.02),
        'norm2_g': 1.0 + nrm(ks[7], (DEPTH, D_MODEL), 0.02),
        'w_in': nrm(ks[8], (DEPTH, D_MODEL, IN_WIDTH), D_MODEL ** -0.5),
        'w_out': nrm(ks[9], (DEPTH, MIX_WIDTH, D_MODEL), MIX_WIDTH ** -0.5),
        'na_rpb': nrm(ks[10], (DEPTH, NA_HEADS, 2 * WIN_ROWS - 1, 2 * WIN_COLS - 1), 0.1),
        'ret_log_decay': ret_log_decay,
        'ret_norm_g': 1.0 + nrm(ks[12], (DEPTH, RET_WIDTH), 0.02),
        'sgu_ln_g': 1.0 + nrm(ks[13], (DEPTH, SGU_WIDTH), 0.02),
        'sgu_ln_b': nrm(ks[14], (DEPTH, SGU_WIDTH), 0.02),
        'sgu_w': nrm(ks[15], (DEPTH, SGU_GROUPS, SGU_CHUNK, SGU_CHUNK), SGU_CHUNK ** -0.5),
        'sgu_b': 1.0 + nrm(ks[16], (DEPTH, SGU_GROUPS, SGU_CHUNK), 0.02),
        'router_w': nrm(ks[17], (DEPTH, D_MODEL, N_EXPERTS), D_MODEL ** -0.5),
        'w_gate': nrm(ks[18], (DEPTH, N_EXPERTS, D_MODEL, EXPERT_FF), D_MODEL ** -0.5),
        'w_up': nrm(ks[19], (DEPTH, N_EXPERTS, D_MODEL, EXPERT_FF), D_MODEL ** -0.5),
        'w_down': nrm(ks[20], (DEPTH, N_EXPERTS, EXPERT_FF, D_MODEL), EXPERT_FF ** -0.5),
        'final_g': 1.0 + nrm(ks[21], (D_MODEL,), 0.02),
    }


def reference(x, c, ctx, c_ctx, ada_w, ada_b, norm1_g, norm2_g, w_in, w_out, na_rpb, ret_log_decay,
              ret_norm_g, sgu_ln_g, sgu_ln_b, sgu_w, sgu_b, router_w, w_gate, w_up, w_down, final_g):
    h_ctx, h_lat = ctx, x
    for i in range(DEPTH):
        lp = {
            'ada_w': ada_w[i], 'ada_b': ada_b[i], 'norm1_g': norm1_g[i], 'norm2_g': norm2_g[i],
            'w_in': w_in[i], 'w_out': w_out[i], 'na_rpb': na_rpb[i], 'ret_log_decay': ret_log_decay[i],
            'ret_norm_g': ret_norm_g[i], 'sgu_ln_g': sgu_ln_g[i], 'sgu_ln_b': sgu_ln_b[i],
            'sgu_w': sgu_w[i], 'sgu_b': sgu_b[i], 'router_w': router_w[i],
            'w_gate': w_gate[i], 'w_up': w_up[i], 'w_down': w_down[i],
        }
        h_ctx, h_lat = hybrid_layer(h_ctx, h_lat, c, c_ctx, lp, update_ctx=(i < DEPTH - 1))
    return rmsnorm(h_lat, final_g)
```

```python
import functools

import numpy as np
import jax
import jax.numpy as jnp
from jax import lax
from jax.experimental import pallas as pl
from jax.experimental.pallas import tpu as pltpu

D_MODEL = 4096
BATCH = 2
SEQ = 4096
DEPTH = 2
GRID_W = 64
GRID_ROWS = SEQ // GRID_W
CTX_LEN = 256
HEAD_DIM = 128
NA_HEADS = 12
RET_HEADS = 12
SGU_GROUPS = 8
SGU_CHUNK = 128
RET_CHUNK = 128
WIN_ROWS = 8
WIN_COLS = 16
NA_WIDTH = NA_HEADS * HEAD_DIM
RET_WIDTH = RET_HEADS * HEAD_DIM
SGU_WIDTH = SGU_GROUPS * HEAD_DIM
MIX_WIDTH = NA_WIDTH + RET_WIDTH + SGU_WIDTH
IN_WIDTH = 3 * NA_WIDTH + 4 * RET_WIDTH + 2 * SGU_WIDTH
N_EXPERTS = 16
EXPERT_FF = 1024
EC_CAPACITY = 2
N_MOD = 6
ROPE_BASE = 10000.0
NORM_EPS = 1e-6

N_CTX = BATCH * CTX_LEN
N_LAT = BATCH * SEQ
N_TOK = N_CTX + N_LAT
MOD_CTX_ROW = BATCH
ATT_SCALE = HEAD_DIM ** -0.5
MASK_VALUE = -1e30

Q_ROWS = 4
Q_BLK = Q_ROWS * GRID_W
K_BAND_ROWS = 12
K_BAND = K_BAND_ROWS * GRID_W
N_QBLK = GRID_ROWS // Q_ROWS

VMEM_LIMIT = 56 * 1024 * 1024

F32 = jnp.float32
BF16 = jnp.bfloat16


def _params(n_axes, vmem=VMEM_LIMIT):
    return pltpu.CompilerParams(dimension_semantics=("arbitrary",) * n_axes, vmem_limit_bytes=vmem)


def _segment(tile_idx, tile_rows):
    n_ctx_tiles = N_CTX // tile_rows
    return jnp.where(tile_idx < n_ctx_tiles, MOD_CTX_ROW, (tile_idx - n_ctx_tiles) // (SEQ // tile_rows))


def _silu(x):
    return x / (1.0 + jnp.exp(-x))


def _gelu_tanh(x):
    return x * (0.5 * (1.0 + jnp.tanh(np.float32(np.sqrt(2.0 / np.pi)) * (x + 0.044715 * (x * x * x)))))


ADA_TN = 512
ADA_ROWS = 16


def _adaln_kernel(c_ref, w_ref, b_ref, o_ref):
    s = _silu(c_ref[...]).astype(BF16)
    o_ref[...] = jnp.dot(s, w_ref[...].astype(BF16), preferred_element_type=F32) + b_ref[...]


def adaln(c_rows, ada_w, ada_b, li):
    n = ada_w.shape[2]
    return pl.pallas_call(
        _adaln_kernel,
        grid=(n // ADA_TN,),
        in_specs=[pl.BlockSpec((ADA_ROWS, D_MODEL), lambda j: (0, 0)),
                  pl.BlockSpec((None, D_MODEL, ADA_TN), lambda j: (li, 0, j)),
                  pl.BlockSpec((1, ADA_TN), lambda j: (0, j))],
        out_specs=pl.BlockSpec((ADA_ROWS, ADA_TN), lambda j: (0, j)),
        out_shape=jax.ShapeDtypeStruct((ADA_ROWS, n), F32),
        compiler_params=_params(1),
        name="adaln",
    )(c_rows, ada_w, ada_b.reshape(1, n))


NORM_TILE = 256


def _rms(x, g):
    return x * lax.rsqrt(jnp.mean(x * x, axis=-1, keepdims=True) + NORM_EPS) * g


def _norm_mod_kernel(x_ref, g_ref, sh_ref, sc_ref, o_ref):
    y = _rms(x_ref[...], g_ref[...])
    o_ref[...] = (y * (1.0 + sc_ref[0]) + sh_ref[0]).astype(o_ref.dtype)


def norm_mod(h, g, mod, shift_k, scale_k, row0):
    t0 = row0 // NORM_TILE
    n_tiles = (N_TOK - row0) // NORM_TILE

    def mod_map(k):
        return lambda i: (_segment(i + t0, NORM_TILE) * N_MOD + k, 0, 0)

    return pl.pallas_call(
        _norm_mod_kernel,
        grid=(n_tiles,),
        in_specs=[pl.BlockSpec((NORM_TILE, D_MODEL), lambda i: (i + t0, 0)),
                  pl.BlockSpec((1, D_MODEL), lambda i: (0, 0)),
                  pl.BlockSpec((1, 1, D_MODEL), mod_map(shift_k)),
                  pl.BlockSpec((1, 1, D_MODEL), mod_map(scale_k))],
        out_specs=pl.BlockSpec((NORM_TILE, D_MODEL), lambda i: (i + t0, 0)),
        out_shape=jax.ShapeDtypeStruct((N_TOK, D_MODEL), BF16),
        compiler_params=_params(1),
        name="norm_mod",
    )(h, g.reshape(1, D_MODEL), mod, mod)


PROJ_TM = 1088
PROJ_TN = 512


def _proj_kernel(a_ref, w_ref, o_ref, wbf_ref):
    @pl.when(pl.program_id(1) == 0)
    def _():
        wbf_ref[...] = w_ref[...].astype(BF16)

    o_ref[...] = jnp.dot(a_ref[...], wbf_ref[...], preferred_element_type=F32)


def in_proj(a, w_in, li):
    return pl.pallas_call(
        _proj_kernel,
        grid=(IN_WIDTH // PROJ_TN, N_TOK // PROJ_TM),
        in_specs=[pl.BlockSpec((PROJ_TM, D_MODEL), lambda n, m: (m, 0)),
                  pl.BlockSpec((None, D_MODEL, PROJ_TN), lambda n, m: (li, 0, n))],
        out_specs=pl.BlockSpec((PROJ_TM, PROJ_TN), lambda n, m: (m, n)),
        out_shape=jax.ShapeDtypeStruct((N_TOK, IN_WIDTH), F32),
        scratch_shapes=[pltpu.VMEM((D_MODEL, PROJ_TN), BF16)],
        compiler_params=_params(2),
        name="in_proj",
    )(a, w_in)


OUT_TM = 512
OUT_TN = 512


def _out_proj_kernel(a_ref, w_ref, h_ref, gate_ref, o_ref, wbf_ref):
    @pl.when(pl.program_id(1) == 0)
    def _():
        wbf_ref[...] = w_ref[...].astype(BF16)

    y = jnp.dot(a_ref[...], wbf_ref[...], preferred_element_type=F32)
    o_ref[...] = h_ref[...] + gate_ref[0] * y


def out_proj_residual(mix, w_out, li, h, mod, gate_k, row0):
    t0 = row0 // OUT_TM
    n_tiles = (N_TOK - row0) // OUT_TM
    return pl.pallas_call(
        _out_proj_kernel,
        grid=(D_MODEL // OUT_TN, n_tiles),
        in_specs=[pl.BlockSpec((OUT_TM, MIX_WIDTH), lambda n, m: (m + t0, 0)),
                  pl.BlockSpec((None, MIX_WIDTH, OUT_TN), lambda n, m: (li, 0, n)),
                  pl.BlockSpec((OUT_TM, OUT_TN), lambda n, m: (m + t0, n)),
                  pl.BlockSpec((1, 1, OUT_TN), lambda n, m: (_segment(m + t0, OUT_TM) * N_MOD + gate_k, 0, n))],
        out_specs=pl.BlockSpec((OUT_TM, OUT_TN), lambda n, m: (m + t0, n)),
        out_shape=jax.ShapeDtypeStruct((N_TOK, D_MODEL), F32),
        scratch_shapes=[pltpu.VMEM((MIX_WIDTH, OUT_TN), BF16)],
        input_output_aliases={2: 0},
        compiler_params=_params(2),
        name="out_proj",
    )(mix, w_out, h, mod)


def _na_kernel(q_ref, k0_ref, k1_ref, k2_ref, kc_ref, v0_ref, v1_ref, v2_ref, vc_ref, bias_ref, o_ref):
    c0 = pl.multiple_of(pl.program_id(2) * HEAD_DIM, HEAD_DIM)
    cols = pl.ds(c0, HEAD_DIM)
    q = q_ref[:, cols].astype(BF16)

    def scores(k_ref):
        k = k_ref[:, cols].astype(BF16)
        return lax.dot_general(q, k, (((1,), (1,)), ((), ())), preferred_element_type=F32) * ATT_SCALE

    s_win = jnp.concatenate([scores(k0_ref), scores(k1_ref), scores(k2_ref)], axis=1) + bias_ref[...]
    s_ctx = scores(kc_ref)
    m = jnp.maximum(jnp.max(s_win, axis=1, keepdims=True), jnp.max(s_ctx, axis=1, keepdims=True))
    e_win = jnp.exp(s_win - m)
    e_ctx = jnp.exp(s_ctx - m)
    denom = jnp.sum(e_win, axis=1, keepdims=True) + jnp.sum(e_ctx, axis=1, keepdims=True)
    acc = jnp.dot(e_ctx.astype(BF16), vc_ref[:, cols].astype(BF16), preferred_element_type=F32)
    for t, v_ref in enumerate((v0_ref, v1_ref, v2_ref)):
        e = e_win[:, t * Q_BLK:(t + 1) * Q_BLK].astype(BF16)
        acc = acc + jnp.dot(e, v_ref[:, cols].astype(BF16), preferred_element_type=F32)
    o_ref[...] = (acc / denom).astype(o_ref.dtype)


def _na_bias_table(rpb):
    row_idx = np.zeros((3, Q_BLK, K_BAND), np.int32)
    col_idx = np.zeros((3, Q_BLK, K_BAND), np.int32)
    valid = np.zeros((3, Q_BLK, K_BAND), bool)
    for pat, blk in enumerate((0, 1, N_QBLK - 1)):
        band0 = int(np.clip(Q_ROWS * blk - Q_ROWS, 0, GRID_ROWS - K_BAND_ROWS))
        qr = Q_ROWS * blk + np.arange(Q_BLK) // GRID_W
        qc = np.arange(Q_BLK) % GRID_W
        kr = band0 + np.arange(K_BAND) // GRID_W
        kc = np.arange(K_BAND) % GRID_W
        r0 = np.clip(qr - WIN_ROWS // 2, 0, GRID_ROWS - WIN_ROWS)
        c0 = np.clip(qc - WIN_COLS // 2, 0, GRID_W - WIN_COLS)
        ok_r = (kr[None, :] >= r0[:, None]) & (kr[None, :] < r0[:, None] + WIN_ROWS)
        ok_c = (kc[None, :] >= c0[:, None]) & (kc[None, :] < c0[:, None] + WIN_COLS)
        valid[pat] = ok_r & ok_c
        row_idx[pat] = np.clip(kr[None, :] - qr[:, None] + (WIN_ROWS - 1), 0, 2 * WIN_ROWS - 2)
        col_idx[pat] = np.clip(kc[None, :] - qc[:, None] + (WIN_COLS - 1), 0, 2 * WIN_COLS - 2)
    bias = rpb.astype(F32)[:, row_idx, col_idx]
    return jnp.where(valid[None], bias, MASK_VALUE)


def na_attention(p, rpb):
    bias = _na_bias_table(rpb)
    ctx_blocks = N_CTX // Q_BLK

    def q_map(b, i, h):
        return (ctx_blocks + b * N_QBLK + i, 0)

    def band_map(t, col):
        def f(b, i, h):
            return (ctx_blocks + b * N_QBLK + jnp.clip(i - 1, 0, N_QBLK - 3) + t, col)
        return f

    def ctx_map(col):
        return lambda b, i, h: (b, col)

    def bias_map(b, i, h):
        return (h, jnp.where(i == 0, 0, jnp.where(i == N_QBLK - 1, 2, 1)), 0, 0)

    blk = (Q_BLK, NA_WIDTH)
    return pl.pallas_call(
        _na_kernel,
        grid=(BATCH, N_QBLK, NA_HEADS),
        in_specs=[pl.BlockSpec(blk, q_map),
                  pl.BlockSpec(blk, band_map(0, 1)), pl.BlockSpec(blk, band_map(1, 1)),
                  pl.BlockSpec(blk, band_map(2, 1)), pl.BlockSpec(blk, ctx_map(1)),
                  pl.BlockSpec(blk, band_map(0, 2)), pl.BlockSpec(blk, band_map(1, 2)),
                  pl.BlockSpec(blk, band_map(2, 2)), pl.BlockSpec(blk, ctx_map(2)),
                  pl.BlockSpec((None, None, Q_BLK, K_BAND), bias_map)],
        out_specs=pl.BlockSpec((Q_BLK, HEAD_DIM), lambda b, i, h: (b * N_QBLK + i, h)),
        out_shape=jax.ShapeDtypeStruct((N_LAT, NA_WIDTH), BF16),
        compiler_params=_params(3),
        name="na_attention",
    )(p, p, p, p, p, p, p, p, p, bias)


def _ctx_attn_kernel(q_ref, k_ref, v_ref, o_ref):
    s = lax.dot_general(q_ref[...].astype(BF16), k_ref[...].astype(BF16), (((1,), (1,)), ((), ())),
                        preferred_element_type=F32) * ATT_SCALE
    e = jnp.exp(s - jnp.max(s, axis=1, keepdims=True))
    acc = jnp.dot(e.astype(BF16), v_ref[...].astype(BF16), preferred_element_type=F32)
    o_ref[...] = (acc / jnp.sum(e, axis=1, keepdims=True)).astype(o_ref.dtype)


def ctx_attention(p):
    blk = (CTX_LEN, HEAD_DIM)
    return pl.pallas_call(
        _ctx_attn_kernel,
        grid=(BATCH, NA_HEADS),
        in_specs=[pl.BlockSpec(blk, lambda b, h: (b, h)),
                  pl.BlockSpec(blk, lambda b, h: (b, NA_HEADS + h)),
                  pl.BlockSpec(blk, lambda b, h: (b, 2 * NA_HEADS + h))],
        out_specs=pl.BlockSpec(blk, lambda b, h: (b, h)),
        out_shape=jax.ShapeDtypeStruct((N_CTX, NA_WIDTH), BF16),
        compiler_params=_params(2),
        name="ctx_attention",
    )(p, p, p)


CTX_CHUNKS = CTX_LEN // RET_CHUNK
LAT_CHUNKS = SEQ // RET_CHUNK
RET_STEPS = CTX_CHUNKS + LAT_CHUNKS
RET_Q_COL, RET_K_COL, RET_V_COL, RET_G_COL = 3, 4, 5, 6


def _ret_kernel(ld_ref, q_ref, k_ref, v_ref, cos_ref, sin_ref, *rest, backward):
    if backward:
        of_ref, g_ref, ng_ref, o_ref, st_ref = rest
    else:
        o_ref, st_ref = rest

    @pl.when(pl.program_id(1) == 0)
    def _():
        st_ref[...] = jnp.zeros_like(st_ref)

    ii = lax.broadcasted_iota(jnp.int32, (RET_CHUNK, RET_CHUNK), 0)
    jj = lax.broadcasted_iota(jnp.int32, (RET_CHUNK, RET_CHUNK), 1)
    diff = ((jj - ii) if backward else (ii - jj)).astype(F32)
    causal = diff >= 0.0
    diff = jnp.maximum(diff, 0.0)
    row = lax.broadcasted_iota(jnp.int32, (RET_CHUNK, 1), 0)
    pos = ((RET_CHUNK - 1 - row) if backward else row).astype(F32)
    first_half = (jj % (HEAD_DIM // 2)) < (HEAD_DIM // 4)
    cos = cos_ref[...]
    sin = sin_ref[...]

    def rope(x):
        partner = jnp.where(first_half, pltpu.roll(x, HEAD_DIM - HEAD_DIM // 4, axis=1),
                            pltpu.roll(x, HEAD_DIM // 4, axis=1))
        return x * cos + partner * sin

    for hh in range(RET_HEADS):
        cols = slice(hh * HEAD_DIM, (hh + 1) * HEAD_DIM)
        ld = ld_ref[hh]
        inner_decay = jnp.where(causal, jnp.exp(diff * ld), 0.0)
        q_decay = jnp.exp((pos + 1.0) * ld)
        k_decay = jnp.exp((RET_CHUNK - 1.0 - pos) * ld)
        chunk_decay = jnp.exp(jnp.full((1, HEAD_DIM), RET_CHUNK, F32) * ld)
        q = rope(q_ref[:, cols])
        k = rope(k_ref[:, cols]) * ATT_SCALE
        qb = q.astype(BF16)
        vb = v_ref[:, cols].astype(BF16)
        att = lax.dot_general(qb, k.astype(BF16), (((1,), (1,)), ((), ())), preferred_element_type=F32)
        inner = jnp.dot((att * inner_decay).astype(BF16), vb, preferred_element_type=F32)
        state = st_ref[hh]
        cross = jnp.dot(qb, state.astype(BF16), preferred_element_type=F32) * q_decay
        kd_t = (k * k_decay).T.astype(BF16)
        st_ref[hh] = state * chunk_decay + jnp.dot(kd_t, vb, preferred_element_type=F32)
        o = inner + cross
        if backward:
            tot = of_ref[:, cols] + o
            y = _rms(tot, ng_ref[:, cols]) * _silu(g_ref[:, cols])
            o_ref[:, cols] = y.astype(o_ref.dtype)
        else:
            o_ref[:, cols] = o


def _ret_chunk_block(b, s, backward):
    if backward:
        ctx_blk = b * CTX_CHUNKS + (CTX_CHUNKS - 1 - s)
        lat_blk = N_CTX // RET_CHUNK + b * LAT_CHUNKS + (LAT_CHUNKS - 1 - (s - CTX_CHUNKS))
    else:
        ctx_blk = b * CTX_CHUNKS + s
        lat_blk = N_CTX // RET_CHUNK + b * LAT_CHUNKS + (s - CTX_CHUNKS)
    return jnp.where(s < CTX_CHUNKS, ctx_blk, lat_blk)


def retention_pass(p, log_decay, cos_t, sin_t, backward, o_fwd=None, norm_g=None):
    def col_map(col):
        return lambda b, s: (_ret_chunk_block(b, s, backward), col)

    wide = (RET_CHUNK, RET_WIDTH)
    in_specs = [pl.BlockSpec(memory_space=pltpu.SMEM),
                pl.BlockSpec(wide, col_map(RET_Q_COL)), pl.BlockSpec(wide, col_map(RET_K_COL)),
                pl.BlockSpec(wide, col_map(RET_V_COL)),
                pl.BlockSpec((RET_CHUNK, HEAD_DIM), col_map(0)), pl.BlockSpec((RET_CHUNK, HEAD_DIM), col_map(0))]
    args = [log_decay, p, p, p, cos_t, sin_t]
    if backward:
        in_specs += [pl.BlockSpec(wide, col_map(0)), pl.BlockSpec(wide, col_map(RET_G_COL)),
                     pl.BlockSpec((1, RET_WIDTH), lambda b, s: (0, 0))]
        args += [o_fwd, p, norm_g.reshape(1, RET_WIDTH)]
    return pl.pallas_call(
        functools.partial(_ret_kernel, backward=backward),
        grid=(BATCH, RET_STEPS),
        in_specs=in_specs,
        out_specs=pl.BlockSpec(wide, col_map(0)),
        out_shape=jax.ShapeDtypeStruct((N_TOK, RET_WIDTH), BF16 if backward else F32),
        scratch_shapes=[pltpu.VMEM((RET_HEADS, HEAD_DIM, HEAD_DIM), F32)],
        compiler_params=_params(2),
        name="retention_bwd" if backward else "retention_fwd",
    )(*args)


def _rope_tables():
    quarter = HEAD_DIM // 4
    inv_freq = ROPE_BASE ** (-jnp.arange(quarter, dtype=F32) / quarter)
    t = jnp.arange(SEQ)
    ang_r = (t // GRID_W).astype(F32)[:, None] * inv_freq[None, :]
    ang_c = (t % GRID_W).astype(F32)[:, None] * inv_freq[None, :]
    cos = jnp.concatenate([jnp.cos(ang_r), jnp.cos(ang_r), jnp.cos(ang_c), jnp.cos(ang_c)], axis=-1)
    sin = jnp.concatenate([-jnp.sin(ang_r), jnp.sin(ang_r), -jnp.sin(ang_c), jnp.sin(ang_c)], axis=-1)
    cos = jnp.concatenate([jnp.ones((N_CTX, HEAD_DIM), F32)] + [cos] * BATCH, axis=0)
    sin = jnp.concatenate([jnp.zeros((N_CTX, HEAD_DIM), F32)] + [sin] * BATCH, axis=0)
    return cos, sin


SGU_TOK = 512
SGU_COLS = 512
SGU_U_COL = (3 * NA_WIDTH + 4 * RET_WIDTH) // SGU_COLS
SGU_V_COL = SGU_U_COL + SGU_WIDTH // SGU_COLS
SGU_GPB = SGU_COLS // HEAD_DIM


def _sgu_kernel(u_ref, v_ref, lng_ref, lnb_ref, w_ref, bs_ref, o_ref):
    for g in range(SGU_GPB):
        cols = slice(g * HEAD_DIM, (g + 1) * HEAD_DIM)
        w = w_ref[g].astype(BF16)
        lng = lng_ref[:, cols]
        lnb = lnb_ref[:, cols]
        bias = bs_ref[:, g:g + 1]
        for c in range(SGU_TOK // SGU_CHUNK):
            rows = slice(c * SGU_CHUNK, (c + 1) * SGU_CHUNK)
            v = _gelu_tanh(v_ref[rows, cols])
            mu = jnp.mean(v, axis=-1, keepdims=True)
            vc = v - mu
            var = jnp.mean(vc * vc, axis=-1, keepdims=True)
            vn = vc * lax.rsqrt(var + NORM_EPS) * lng + lnb
            s = jnp.dot(w, vn.astype(BF16), preferred_element_type=F32) + bias
            o_ref[rows, cols] = (_gelu_tanh(u_ref[rows, cols]) * s).astype(o_ref.dtype)


def spatial_gating(p, ln_g, ln_b, w_s, b_s, row0):
    t0 = row0 // SGU_TOK
    n_tiles = (N_TOK - row0) // SGU_TOK
    n_half = SGU_WIDTH // SGU_COLS
    bs_t = b_s.reshape(n_half, SGU_GPB, SGU_CHUNK).transpose(0, 2, 1)
    return pl.pallas_call(
        _sgu_kernel,
        grid=(n_tiles, n_half),
        in_specs=[pl.BlockSpec((SGU_TOK, SGU_COLS), lambda t, c: (t + t0, SGU_U_COL + c)),
                  pl.BlockSpec((SGU_TOK, SGU_COLS), lambda t, c: (t + t0, SGU_V_COL + c)),
                  pl.BlockSpec((1, SGU_COLS), lambda t, c: (0, c)),
                  pl.BlockSpec((1, SGU_COLS), lambda t, c: (0, c)),
                  pl.BlockSpec((SGU_GPB, SGU_CHUNK, SGU_CHUNK), lambda t, c: (c, 0, 0)),
                  pl.BlockSpec((None, SGU_CHUNK, SGU_GPB), lambda t, c: (c, 0, 0))],
        out_specs=pl.BlockSpec((SGU_TOK, SGU_COLS), lambda t, c: (t + t0, c)),
        out_shape=jax.ShapeDtypeStruct((N_TOK, SGU_WIDTH), BF16),
        compiler_params=_params(2),
        name="spatial_gating",
    )(p, p, ln_g.reshape(1, SGU_WIDTH), ln_b.reshape(1, SGU_WIDTH), w_s, bs_t)


ROUTER_LANES = 128


def _norm_router_kernel(x_ref, g_ref, sh_ref, sc_ref, rw_ref, f_ref, aff_ref):
    y = _rms(x_ref[...], g_ref[...])
    f = y * (1.0 + sc_ref[0]) + sh_ref[0]
    fb = f.astype(BF16)
    f_ref[...] = fb
    logits = jnp.dot(fb, rw_ref[...].astype(BF16), preferred_element_type=F32)
    lane = lax.broadcasted_iota(jnp.int32, logits.shape, 1)
    logits = jnp.where(lane < N_EXPERTS, logits, MASK_VALUE)
    e = jnp.exp(logits - jnp.max(logits, axis=1, keepdims=True))
    aff_ref[...] = e / jnp.sum(e, axis=1, keepdims=True)


def norm_router(h, g, mod, shift_k, scale_k, router_w, row0):
    t0 = row0 // NORM_TILE
    n_tiles = (N_TOK - row0) // NORM_TILE
    rw = jnp.pad(router_w, ((0, 0), (0, ROUTER_LANES - N_EXPERTS)))

    def mod_map(k):
        return lambda i: (_segment(i + t0, NORM_TILE) * N_MOD + k, 0, 0)

    return pl.pallas_call(
        _norm_router_kernel,
        grid=(n_tiles,),
        in_specs=[pl.BlockSpec((NORM_TILE, D_MODEL), lambda i: (i + t0, 0)),
                  pl.BlockSpec((1, D_MODEL), lambda i: (0, 0)),
                  pl.BlockSpec((1, 1, D_MODEL), mod_map(shift_k)),
                  pl.BlockSpec((1, 1, D_MODEL), mod_map(scale_k)),
                  pl.BlockSpec((D_MODEL, ROUTER_LANES), lambda i: (0, 0))],
        out_specs=[pl.BlockSpec((NORM_TILE, D_MODEL), lambda i: (i + t0, 0)),
                   pl.BlockSpec((NORM_TILE, ROUTER_LANES), lambda i: (i + t0, 0))],
        out_shape=[jax.ShapeDtypeStruct((N_TOK, D_MODEL), BF16),
                   jax.ShapeDtypeStruct((N_TOK, ROUTER_LANES), F32)],
        compiler_params=_params(1),
        name="norm_router",
    )(h, g.reshape(1, D_MODEL), mod, mod, rw)


FF_CHUNK = 256
DOWN_TN = 512


def _ffn_up_kernel(x_ref, wg_ref, wu_ref, o_ref):
    x = x_ref[...]
    gate = jnp.dot(x, wg_ref[...].astype(BF16), preferred_element_type=F32)
    up = jnp.dot(x, wu_ref[...].astype(BF16), preferred_element_type=F32)
    o_ref[...] = (_silu(gate) * up).astype(o_ref.dtype)


def _ffn_down_kernel(h_ref, wd_ref, o_ref):
    o_ref[...] = jnp.dot(h_ref[...], wd_ref[...].astype(BF16), preferred_element_type=F32)


def expert_ffn(xin, w_gate, w_up, w_down, li):
    r = xin.shape[1]
    hid = pl.pallas_call(
        _ffn_up_kernel,
        grid=(N_EXPERTS, EXPERT_FF // FF_CHUNK),
        in_specs=[pl.BlockSpec((None, r, D_MODEL), lambda e, f: (e, 0, 0)),
                  pl.BlockSpec((None, None, D_MODEL, FF_CHUNK), lambda e, f: (li, e, 0, f)),
                  pl.BlockSpec((None, None, D_MODEL, FF_CHUNK), lambda e, f: (li, e, 0, f))],
        out_specs=pl.BlockSpec((None, r, FF_CHUNK), lambda e, f: (e, 0, f)),
        out_shape=jax.ShapeDtypeStruct((N_EXPERTS, r, EXPERT_FF), BF16),
        compiler_params=_params(2),
        name="ffn_up",
    )(xin, w_gate, w_up)
    return pl.pallas_call(
        _ffn_down_kernel,
        grid=(N_EXPERTS, D_MODEL // DOWN_TN),
        in_specs=[pl.BlockSpec((None, r, EXPERT_FF), lambda e, n: (e, 0, 0)),
                  pl.BlockSpec((None, None, EXPERT_FF, DOWN_TN), lambda e, n: (li, e, 0, n))],
        out_specs=pl.BlockSpec((None, r, DOWN_TN), lambda e, n: (e, 0, n)),
        out_shape=jax.ShapeDtypeStruct((N_EXPERTS, r, D_MODEL), F32),
        compiler_params=_params(2),
        name="ffn_down",
    )(hid, w_down)


def _final_norm_kernel(x_ref, g_ref, o_ref):
    o_ref[...] = _rms(x_ref[...], g_ref[...])


def final_norm(h, g):
    t0 = N_CTX // NORM_TILE
    return pl.pallas_call(
        _final_norm_kernel,
        grid=(N_LAT // NORM_TILE,),
        in_specs=[pl.BlockSpec((NORM_TILE, D_MODEL), lambda i: (i + t0, 0)),
                  pl.BlockSpec((1, D_MODEL), lambda i: (0, 0))],
        out_specs=pl.BlockSpec((NORM_TILE, D_MODEL), lambda i: (i, 0)),
        out_shape=jax.ShapeDtypeStruct((N_LAT, D_MODEL), F32),
        compiler_params=_params(1),
        name="final_norm",
    )(h, g.reshape(1, D_MODEL))


def _route(aff, n_per_sample, row0):
    cap = EC_CAPACITY * n_per_sample // N_EXPERTS
    a = aff[row0:row0 + BATCH * n_per_sample, :N_EXPERTS].reshape(BATCH, n_per_sample, N_EXPERTS)
    gates, idx = lax.top_k(a.transpose(0, 2, 1), cap)
    rows = idx + (row0 + jnp.arange(BATCH) * n_per_sample)[:, None, None]
    return rows, gates


def _moe(h, f, aff, mod, big, li, update_ctx):
    rows, gates = _route(aff, SEQ, N_CTX)
    rows = rows.transpose(1, 0, 2).reshape(N_EXPERTS, -1)
    gates = gates.transpose(1, 0, 2).reshape(N_EXPERTS, -1)
    if update_ctx:
        rows_c, gates_c = _route(aff, CTX_LEN, 0)
        rows = jnp.concatenate([rows_c.transpose(1, 0, 2).reshape(N_EXPERTS, -1), rows], axis=1)
        gates = jnp.concatenate([gates_c.transpose(1, 0, 2).reshape(N_EXPERTS, -1), gates], axis=1)
    y = expert_ffn(f[rows], big['w_gate'], big['w_up'], big['w_down'], li) * gates[..., None]
    moe = jnp.zeros((N_TOK, D_MODEL), F32).at[rows.reshape(-1)].add(y.reshape(-1, D_MODEL))
    seg = jnp.concatenate([jnp.full((N_CTX,), MOD_CTX_ROW), jnp.repeat(jnp.arange(BATCH), SEQ)])
    gate_f = mod.reshape(3, N_MOD, D_MODEL)[:, 5][seg]
    return h + gate_f * moe


def _layer(h, c_rows, big, li, lp, cos_t, sin_t, update_ctx):
    row0 = 0 if update_ctx else N_CTX
    mod = adaln(c_rows, big['ada_w'], lp['ada_b'], li)[:3].reshape(3 * N_MOD, 1, D_MODEL)
    a = norm_mod(h, lp['norm1_g'], mod, 0, 1, 0)
    p = in_proj(a, big['w_in'], li)
    o_a = na_attention(p, lp['na_rpb'])
    if update_ctx:
        o_a = jnp.concatenate([ctx_attention(p), o_a], axis=0)
    else:
        o_a = jnp.concatenate([jnp.zeros((N_CTX, NA_WIDTH), BF16), o_a], axis=0)
    o_f = retention_pass(p, lp['ret_log_decay'][0], cos_t, sin_t, backward=False)
    o_b = retention_pass(p, lp['ret_log_decay'][1], cos_t, sin_t, backward=True, o_fwd=o_f,
                         norm_g=lp['ret_norm_g'])
    o_c = spatial_gating(p, lp['sgu_ln_g'], lp['sgu_ln_b'], lp['sgu_w'], lp['sgu_b'], row0)
    mix = jnp.concatenate([o_a, o_b, o_c], axis=-1)
    h = out_proj_residual(mix, big['w_out'], li, h, mod, 2, row0)
    f, aff = norm_router(h, lp['norm2_g'], mod, 3, 4, lp['router_w'], row0)
    return _moe(h, f, aff, mod, big, li, update_ctx)


def kernel(x, c, ctx, c_ctx, ada_w, ada_b, norm1_g, norm2_g, w_in, w_out, na_rpb, ret_log_decay, ret_norm_g,
           sgu_ln_g, sgu_ln_b, sgu_w, sgu_b, router_w, w_gate, w_up, w_down, final_g):
    h = jnp.concatenate([ctx.reshape(N_CTX, D_MODEL), x.reshape(N_LAT, D_MODEL)], axis=0)
    c_rows = jnp.zeros((ADA_ROWS, D_MODEL), F32).at[:BATCH].set(c).at[MOD_CTX_ROW].set(c_ctx)
    cos_t, sin_t = _rope_tables()
    big = {'ada_w': ada_w, 'w_in': w_in, 'w_out': w_out, 'w_gate': w_gate, 'w_up': w_up, 'w_down': w_down}
    for i in range(DEPTH):
        lp = {
            'ada_b': ada_b[i], 'norm1_g': norm1_g[i], 'norm2_g': norm2_g[i],
            'na_rpb': na_rpb[i], 'ret_log_decay': ret_log_decay[i],
            'ret_norm_g': ret_norm_g[i], 'sgu_ln_g': sgu_ln_g[i], 'sgu_ln_b': sgu_ln_b[i],
            'sgu_w': sgu_w[i], 'sgu_b': sgu_b[i], 'router_w': router_w[i],
        }
        h = _layer(h, c_rows, big, i, lp, cos_t, sin_t, update_ctx=(i < DEPTH - 1))
    return final_norm(h, final_g).reshape(BATCH, SEQ, D_MODEL)
```

```python
import functools

import numpy as np
import jax
import jax.numpy as jnp
from jax import lax
from jax.experimental import pallas as pl
from jax.experimental.pallas import tpu as pltpu

D_MODEL = 4096
BATCH = 2
SEQ = 4096
DEPTH = 2
GRID_W = 64
GRID_ROWS = SEQ // GRID_W
CTX_LEN = 256
HEAD_DIM = 128
NA_HEADS = 12
RET_HEADS = 12
SGU_GROUPS = 8
SGU_CHUNK = 128
RET_CHUNK = 128
WIN_ROWS = 8
WIN_COLS = 16
NA_WIDTH = NA_HEADS * HEAD_DIM
RET_WIDTH = RET_HEADS * HEAD_DIM
SGU_WIDTH = SGU_GROUPS * HEAD_DIM
MIX_WIDTH = NA_WIDTH + RET_WIDTH + SGU_WIDTH
IN_WIDTH = 3 * NA_WIDTH + 4 * RET_WIDTH + 2 * SGU_WIDTH
N_EXPERTS = 16
EXPERT_FF = 1024
EC_CAPACITY = 2
N_MOD = 6
ROPE_BASE = 10000.0
NORM_EPS = 1e-6

N_CTX = BATCH * CTX_LEN
N_LAT = BATCH * SEQ
N_TOK = N_CTX + N_LAT
MOD_CTX_ROW = BATCH
ATT_SCALE = HEAD_DIM ** -0.5
MASK_VALUE = -1e30

Q_ROWS = 4
Q_BLK = Q_ROWS * GRID_W
K_BAND_ROWS = 12
K_BAND = K_BAND_ROWS * GRID_W
N_QBLK = GRID_ROWS // Q_ROWS

VMEM_LIMIT = 56 * 1024 * 1024

F32 = jnp.float32
BF16 = jnp.bfloat16


def _params(n_axes, vmem=VMEM_LIMIT):
    return pltpu.CompilerParams(dimension_semantics=("arbitrary",) * n_axes, vmem_limit_bytes=vmem)


def _segment(tile_idx, tile_rows):
    n_ctx_tiles = N_CTX // tile_rows
    return jnp.where(tile_idx < n_ctx_tiles, MOD_CTX_ROW, (tile_idx - n_ctx_tiles) // (SEQ // tile_rows))


def _silu(x):
    return x / (1.0 + jnp.exp(-x))


def _gelu_tanh(x):
    return x * (0.5 * (1.0 + jnp.tanh(np.float32(np.sqrt(2.0 / np.pi)) * (x + 0.044715 * (x * x * x)))))


ADA_TN = 512
ADA_ROWS = 16


def _adaln_kernel(c_ref, w_ref, b_ref, o_ref):
    s = _silu(c_ref[...]).astype(BF16)
    o_ref[...] = jnp.dot(s, w_ref[...].astype(BF16), preferred_element_type=F32) + b_ref[...]


def adaln(c_rows, ada_w, ada_b, li):
    n = ada_w.shape[2]
    return pl.pallas_call(
        _adaln_kernel,
        grid=(n // ADA_TN,),
        in_specs=[pl.BlockSpec((ADA_ROWS, D_MODEL), lambda j: (0, 0)),
                  pl.BlockSpec((None, D_MODEL, ADA_TN), lambda j: (li, 0, j)),
                  pl.BlockSpec((1, ADA_TN), lambda j: (0, j))],
        out_specs=pl.BlockSpec((ADA_ROWS, ADA_TN), lambda j: (0, j)),
        out_shape=jax.ShapeDtypeStruct((ADA_ROWS, n), F32),
        compiler_params=_params(1),
        name="adaln",
    )(c_rows, ada_w, ada_b.reshape(1, n))


NORM_TILE = 256


def _rms(x, g):
    return x * lax.rsqrt(jnp.mean(x * x, axis=-1, keepdims=True) + NORM_EPS) * g


def _norm_mod_kernel(x_ref, g_ref, sh_ref, sc_ref, o_ref):
    y = _rms(x_ref[...], g_ref[...])
    o_ref[...] = (y * (1.0 + sc_ref[0]) + sh_ref[0]).astype(o_ref.dtype)


def norm_mod(h, g, mod, shift_k, scale_k, row0):
    t0 = row0 // NORM_TILE
    n_tiles = (N_TOK - row0) // NORM_TILE

    def mod_map(k):
        return lambda i: (_segment(i + t0, NORM_TILE) * N_MOD + k, 0, 0)

    return pl.pallas_call(
        _norm_mod_kernel,
        grid=(n_tiles,),
        in_specs=[pl.BlockSpec((NORM_TILE, D_MODEL), lambda i: (i + t0, 0)),
                  pl.BlockSpec((1, D_MODEL), lambda i: (0, 0)),
                  pl.BlockSpec((1, 1, D_MODEL), mod_map(shift_k)),
                  pl.BlockSpec((1, 1, D_MODEL), mod_map(scale_k))],
        out_specs=pl.BlockSpec((NORM_TILE, D_MODEL), lambda i: (i + t0, 0)),
        out_shape=jax.ShapeDtypeStruct((N_TOK, D_MODEL), BF16),
        compiler_params=_params(1),
        name="norm_mod",
    )(h, g.reshape(1, D_MODEL), mod, mod)


PROJ_TM = 1088
PROJ_TN = 512


def _proj_kernel(a_ref, w_ref, o_ref, wbf_ref):
    @pl.when(pl.program_id(1) == 0)
    def _():
        wbf_ref[...] = w_ref[...].astype(BF16)

    o_ref[...] = jnp.dot(a_ref[...], wbf_ref[...], preferred_element_type=F32)


def in_proj(a, w_in, li):
    return pl.pallas_call(
        _proj_kernel,
        grid=(IN_WIDTH // PROJ_TN, N_TOK // PROJ_TM),
        in_specs=[pl.BlockSpec((PROJ_TM, D_MODEL), lambda n, m: (m, 0)),
                  pl.BlockSpec((None, D_MODEL, PROJ_TN), lambda n, m: (li, 0, n))],
        out_specs=pl.BlockSpec((PROJ_TM, PROJ_TN), lambda n, m: (m, n)),
        out_shape=jax.ShapeDtypeStruct((N_TOK, IN_WIDTH), F32),
        scratch_shapes=[pltpu.VMEM((D_MODEL, PROJ_TN), BF16)],
        compiler_params=_params(2),
        name="in_proj",
    )(a, w_in)


OUT_TM = 512
OUT_TN = 512


def _out_proj_kernel(oa_ref, ob_ref, oc_ref, w_ref, h_ref, gate_ref, o_ref, wbf_ref):
    @pl.when(pl.program_id(1) == 0)
    def _():
        wbf_ref[...] = w_ref[...].astype(BF16)

    y = jnp.dot(oa_ref[...], wbf_ref[0:NA_WIDTH], preferred_element_type=F32)
    y = y + jnp.dot(ob_ref[...], wbf_ref[NA_WIDTH:NA_WIDTH + RET_WIDTH], preferred_element_type=F32)
    y = y + jnp.dot(oc_ref[...], wbf_ref[NA_WIDTH + RET_WIDTH:MIX_WIDTH], preferred_element_type=F32)
    o_ref[...] = h_ref[...] + gate_ref[0] * y


def out_proj_residual(o_a, o_b, o_c, w_out, li, h, mod, gate_k, row0):
    t0 = row0 // OUT_TM
    n_tiles = (N_TOK - row0) // OUT_TM
    return pl.pallas_call(
        _out_proj_kernel,
        grid=(D_MODEL // OUT_TN, n_tiles),
        in_specs=[pl.BlockSpec((OUT_TM, NA_WIDTH), lambda n, m: (m, 0)),
                  pl.BlockSpec((OUT_TM, RET_WIDTH), lambda n, m: (m + t0, 0)),
                  pl.BlockSpec((OUT_TM, SGU_WIDTH), lambda n, m: (m, 0)),
                  pl.BlockSpec((None, MIX_WIDTH, OUT_TN), lambda n, m: (li, 0, n)),
                  pl.BlockSpec((OUT_TM, OUT_TN), lambda n, m: (m + t0, n)),
                  pl.BlockSpec((1, 1, OUT_TN), lambda n, m: (_segment(m + t0, OUT_TM) * N_MOD + gate_k, 0, n))],
        out_specs=pl.BlockSpec((OUT_TM, OUT_TN), lambda n, m: (m + t0, n)),
        out_shape=jax.ShapeDtypeStruct((N_TOK, D_MODEL), F32),
        scratch_shapes=[pltpu.VMEM((MIX_WIDTH, OUT_TN), BF16)],
        input_output_aliases={4: 0},
        compiler_params=_params(2),
        name="out_proj",
    )(o_a, o_b, o_c, w_out, h, mod)


def _na_kernel(q_ref, k0_ref, k1_ref, k2_ref, kc_ref, v0_ref, v1_ref, v2_ref, vc_ref, bias_ref, o_ref):
    c0 = pl.multiple_of(pl.program_id(2) * HEAD_DIM, HEAD_DIM)
    cols = pl.ds(c0, HEAD_DIM)
    q = q_ref[:, cols].astype(BF16)

    def scores(k_ref):
        k = k_ref[:, cols].astype(BF16)
        return lax.dot_general(q, k, (((1,), (1,)), ((), ())), preferred_element_type=F32) * ATT_SCALE

    bias = bias_ref[pl.program_id(2)]
    s_win = jnp.concatenate([scores(k0_ref), scores(k1_ref), scores(k2_ref)], axis=1) + bias
    s_ctx = scores(kc_ref)
    m = jnp.maximum(jnp.max(s_win, axis=1, keepdims=True), jnp.max(s_ctx, axis=1, keepdims=True))
    e_win = jnp.exp(s_win - m)
    e_ctx = jnp.exp(s_ctx - m)
    denom = jnp.sum(e_win, axis=1, keepdims=True) + jnp.sum(e_ctx, axis=1, keepdims=True)
    acc = jnp.dot(e_ctx.astype(BF16), vc_ref[:, cols].astype(BF16), preferred_element_type=F32)
    for t, v_ref in enumerate((v0_ref, v1_ref, v2_ref)):
        e = e_win[:, t * Q_BLK:(t + 1) * Q_BLK].astype(BF16)
        acc = acc + jnp.dot(e, v_ref[:, cols].astype(BF16), preferred_element_type=F32)
    o_ref[...] = (acc / denom).astype(o_ref.dtype)


def _na_bias_table(rpb):
    n_off = 2 * WIN_ROWS - 1
    pad = GRID_W - WIN_COLS
    ext = jnp.pad(rpb.astype(F32), ((0, 0), (0, 0), (pad, pad)))
    toep = jnp.stack([ext[..., GRID_W - 1 - qc:2 * GRID_W - 1 - qc] for qc in range(GRID_W)], axis=-2)
    qc = np.arange(GRID_W)
    c0 = np.clip(qc - WIN_COLS // 2, 0, GRID_W - WIN_COLS)
    ok_c = (qc[None, :] >= c0[:, None]) & (qc[None, :] < c0[:, None] + WIN_COLS)
    toep = jnp.where(ok_c, toep, MASK_VALUE)
    toep = jnp.concatenate([toep, jnp.full((NA_HEADS, 1, GRID_W, GRID_W), MASK_VALUE, F32)], axis=1)
    pats = []
    for blk in (0, 1, N_QBLK - 1):
        band0 = int(np.clip(Q_ROWS * blk - Q_ROWS, 0, GRID_ROWS - K_BAND_ROWS))
        rows = []
        for q in range(Q_ROWS):
            qr = Q_ROWS * blk + q
            r0 = int(np.clip(qr - WIN_ROWS // 2, 0, GRID_ROWS - WIN_ROWS))
            blocks = []
            for k in range(K_BAND_ROWS):
                kr = band0 + k
                off = kr - qr + (WIN_ROWS - 1) if r0 <= kr < r0 + WIN_ROWS else n_off
                blocks.append(toep[:, off])
            rows.append(jnp.concatenate(blocks, axis=-1))
        pats.append(jnp.concatenate(rows, axis=-2))
    pats.append(jnp.full((NA_HEADS, Q_BLK, K_BAND), MASK_VALUE, F32))
    return jnp.stack(pats, axis=0)


def na_attention(p, rpb, with_ctx):
    bias = _na_bias_table(rpb)
    ctx_blocks = N_CTX // Q_BLK
    lead = 1 if with_ctx else 0

    def lat_block(b, j):
        return ctx_blocks + b * N_QBLK + j

    def q_map(b, i, h):
        return (jnp.where(i < lead, b, lat_block(b, i - lead)), 0)

    def band_map(t, col):
        def f(b, i, h):
            return (lat_block(b, jnp.clip(i - lead - 1, 0, N_QBLK - 3)) + t, col)
        return f

    def ctx_map(col):
        return lambda b, i, h: (b, col)

    def bias_map(b, i, h):
        j = i - lead
        return (jnp.where(j < 0, 3, jnp.where(j == 0, 0, jnp.where(j == N_QBLK - 1, 2, 1))), 0, 0, 0)

    def out_map(b, i, h):
        if with_ctx:
            return (jnp.where(i < lead, b, lat_block(b, i - lead)), h)
        return (b * N_QBLK + i, h)

    blk = (Q_BLK, NA_WIDTH)
    return pl.pallas_call(
        _na_kernel,
        grid=(BATCH, N_QBLK + lead, NA_HEADS),
        in_specs=[pl.BlockSpec(blk, q_map),
                  pl.BlockSpec(blk, band_map(0, 1)), pl.BlockSpec(blk, band_map(1, 1)),
                  pl.BlockSpec(blk, band_map(2, 1)), pl.BlockSpec(blk, ctx_map(1)),
                  pl.BlockSpec(blk, band_map(0, 2)), pl.BlockSpec(blk, band_map(1, 2)),
                  pl.BlockSpec(blk, band_map(2, 2)), pl.BlockSpec(blk, ctx_map(2)),
                  pl.BlockSpec((None, NA_HEADS, Q_BLK, K_BAND), bias_map)],
        out_specs=pl.BlockSpec((Q_BLK, HEAD_DIM), out_map),
        out_shape=jax.ShapeDtypeStruct((N_TOK if with_ctx else N_LAT, NA_WIDTH), BF16),
        compiler_params=_params(3),
        name="na_attention",
    )(p, p, p, p, p, p, p, p, p, bias)


CTX_CHUNKS = CTX_LEN // RET_CHUNK
LAT_CHUNKS = SEQ // RET_CHUNK
RET_STEPS = CTX_CHUNKS + LAT_CHUNKS
RET_Q_COL, RET_K_COL, RET_V_COL, RET_G_COL = 3, 4, 5, 6


def _ret_kernel(ld_ref, q_ref, k_ref, v_ref, cos_ref, sin_ref, *rest, backward):
    if backward:
        of_ref, g_ref, ng_ref, o_ref, st_ref = rest
    else:
        o_ref, st_ref = rest

    @pl.when(pl.program_id(1) == 0)
    def _():
        st_ref[...] = jnp.zeros_like(st_ref)

    ii = lax.broadcasted_iota(jnp.int32, (RET_CHUNK, RET_CHUNK), 0)
    jj = lax.broadcasted_iota(jnp.int32, (RET_CHUNK, RET_CHUNK), 1)
    diff = ((jj - ii) if backward else (ii - jj)).astype(F32)
    causal = diff >= 0.0
    diff = jnp.maximum(diff, 0.0)
    row = lax.broadcasted_iota(jnp.int32, (RET_CHUNK, 1), 0)
    pos = ((RET_CHUNK - 1 - row) if backward else row).astype(F32)
    first_half = (jj % (HEAD_DIM // 2)) < (HEAD_DIM // 4)
    cos = cos_ref[...]
    sin = sin_ref[...]

    def rope(x):
        partner = jnp.where(first_half, pltpu.roll(x, HEAD_DIM - HEAD_DIM // 4, axis=1),
                            pltpu.roll(x, HEAD_DIM // 4, axis=1))
        return x * cos + partner * sin

    for hh in range(RET_HEADS):
        cols = slice(hh * HEAD_DIM, (hh + 1) * HEAD_DIM)
        ld = ld_ref[hh]
        inner_decay = jnp.where(causal, jnp.exp(diff * ld), 0.0)
        q_decay = jnp.exp((pos + 1.0) * ld)
        k_decay = jnp.exp((RET_CHUNK - 1.0 - pos) * ld)
        chunk_decay = jnp.exp(jnp.full((1, HEAD_DIM), RET_CHUNK, F32) * ld)
        q = rope(q_ref[:, cols])
        k = rope(k_ref[:, cols]) * ATT_SCALE
        qb = q.astype(BF16)
        vb = v_ref[:, cols].astype(BF16)
        att = lax.dot_general(qb, k.astype(BF16), (((1,), (1,)), ((), ())), preferred_element_type=F32)
        inner = jnp.dot((att * inner_decay).astype(BF16), vb, preferred_element_type=F32)
        state = st_ref[hh]
        cross = jnp.dot(qb, state.astype(BF16), preferred_element_type=F32) * q_decay
        kd_t = (k * k_decay).T.astype(BF16)
        st_ref[hh] = state * chunk_decay + jnp.dot(kd_t, vb, preferred_element_type=F32)
        o = inner + cross
        if backward:
            tot = of_ref[:, cols] + o
            y = _rms(tot, ng_ref[:, cols]) * _silu(g_ref[:, cols])
            o_ref[:, cols] = y.astype(o_ref.dtype)
        else:
            o_ref[:, cols] = o


def _ret_chunk_block(b, s, backward):
    if backward:
        ctx_blk = b * CTX_CHUNKS + (CTX_CHUNKS - 1 - s)
        lat_blk = N_CTX // RET_CHUNK + b * LAT_CHUNKS + (LAT_CHUNKS - 1 - (s - CTX_CHUNKS))
    else:
        ctx_blk = b * CTX_CHUNKS + s
        lat_blk = N_CTX // RET_CHUNK + b * LAT_CHUNKS + (s - CTX_CHUNKS)
    return jnp.where(s < CTX_CHUNKS, ctx_blk, lat_blk)


def retention_pass(p, log_decay, cos_t, sin_t, backward, o_fwd=None, norm_g=None):
    def col_map(col):
        return lambda b, s: (_ret_chunk_block(b, s, backward), col)

    wide = (RET_CHUNK, RET_WIDTH)
    in_specs = [pl.BlockSpec(memory_space=pltpu.SMEM),
                pl.BlockSpec(wide, col_map(RET_Q_COL)), pl.BlockSpec(wide, col_map(RET_K_COL)),
                pl.BlockSpec(wide, col_map(RET_V_COL)),
                pl.BlockSpec((RET_CHUNK, HEAD_DIM), col_map(0)), pl.BlockSpec((RET_CHUNK, HEAD_DIM), col_map(0))]
    args = [log_decay, p, p, p, cos_t, sin_t]
    if backward:
        in_specs += [pl.BlockSpec(wide, col_map(0)), pl.BlockSpec(wide, col_map(RET_G_COL)),
                     pl.BlockSpec((1, RET_WIDTH), lambda b, s: (0, 0))]
        args += [o_fwd, p, norm_g.reshape(1, RET_WIDTH)]
    return pl.pallas_call(
        functools.partial(_ret_kernel, backward=backward),
        grid=(BATCH, RET_STEPS),
        in_specs=in_specs,
        out_specs=pl.BlockSpec(wide, col_map(0)),
        out_shape=jax.ShapeDtypeStruct((N_TOK, RET_WIDTH), BF16 if backward else F32),
        scratch_shapes=[pltpu.VMEM((RET_HEADS, HEAD_DIM, HEAD_DIM), F32)],
        compiler_params=_params(2),
        name="retention_bwd" if backward else "retention_fwd",
    )(*args)


def _rope_tables():
    quarter = HEAD_DIM // 4
    inv_freq = ROPE_BASE ** (-jnp.arange(quarter, dtype=F32) / quarter)
    t = jnp.arange(SEQ)
    ang_r = (t // GRID_W).astype(F32)[:, None] * inv_freq[None, :]
    ang_c = (t % GRID_W).astype(F32)[:, None] * inv_freq[None, :]
    cos = jnp.concatenate([jnp.cos(ang_r), jnp.cos(ang_r), jnp.cos(ang_c), jnp.cos(ang_c)], axis=-1)
    sin = jnp.concatenate([-jnp.sin(ang_r), jnp.sin(ang_r), -jnp.sin(ang_c), jnp.sin(ang_c)], axis=-1)
    cos = jnp.concatenate([jnp.ones((N_CTX, HEAD_DIM), F32)] + [cos] * BATCH, axis=0)
    sin = jnp.concatenate([jnp.zeros((N_CTX, HEAD_DIM), F32)] + [sin] * BATCH, axis=0)
    return cos, sin


SGU_TOK = 512
SGU_COLS = 512
SGU_U_COL = (3 * NA_WIDTH + 4 * RET_WIDTH) // SGU_COLS
SGU_V_COL = SGU_U_COL + SGU_WIDTH // SGU_COLS
SGU_GPB = SGU_COLS // HEAD_DIM


def _sgu_kernel(u_ref, v_ref, lng_ref, lnb_ref, w_ref, bs_ref, o_ref):
    for g in range(SGU_GPB):
        cols = slice(g * HEAD_DIM, (g + 1) * HEAD_DIM)
        w = w_ref[g].astype(BF16)
        lng = lng_ref[:, cols]
        lnb = lnb_ref[:, cols]
        bias = bs_ref[:, g:g + 1]
        for c in range(SGU_TOK // SGU_CHUNK):
            rows = slice(c * SGU_CHUNK, (c + 1) * SGU_CHUNK)
            v = _gelu_tanh(v_ref[rows, cols])
            mu = jnp.mean(v, axis=-1, keepdims=True)
            vc = v - mu
            var = jnp.mean(vc * vc, axis=-1, keepdims=True)
            vn = vc * lax.rsqrt(var + NORM_EPS) * lng + lnb
            s = jnp.dot(w, vn.astype(BF16), preferred_element_type=F32) + bias
            o_ref[rows, cols] = (_gelu_tanh(u_ref[rows, cols]) * s).astype(o_ref.dtype)


def spatial_gating(p, ln_g, ln_b, w_s, b_s, row0):
    t0 = row0 // SGU_TOK
    n_tiles = (N_TOK - row0) // SGU_TOK
    n_half = SGU_WIDTH // SGU_COLS
    bs_t = b_s.reshape(n_half, SGU_GPB, SGU_CHUNK).transpose(0, 2, 1)
    return pl.pallas_call(
        _sgu_kernel,
        grid=(n_tiles, n_half),
        in_specs=[pl.BlockSpec((SGU_TOK, SGU_COLS), lambda t, c: (t + t0, SGU_U_COL + c)),
                  pl.BlockSpec((SGU_TOK, SGU_COLS), lambda t, c: (t + t0, SGU_V_COL + c)),
                  pl.BlockSpec((1, SGU_COLS), lambda t, c: (0, c)),
                  pl.BlockSpec((1, SGU_COLS), lambda t, c: (0, c)),
                  pl.BlockSpec((SGU_GPB, SGU_CHUNK, SGU_CHUNK), lambda t, c: (c, 0, 0)),
                  pl.BlockSpec((None, SGU_CHUNK, SGU_GPB), lambda t, c: (c, 0, 0))],
        out_specs=pl.BlockSpec((SGU_TOK, SGU_COLS), lambda t, c: (t, c)),
        out_shape=jax.ShapeDtypeStruct((N_TOK - row0, SGU_WIDTH), BF16),
        compiler_params=_params(2),
        name="spatial_gating",
    )(p, p, ln_g.reshape(1, SGU_WIDTH), ln_b.reshape(1, SGU_WIDTH), w_s, bs_t)


ROUTER_LANES = 128


def _norm_router_kernel(x_ref, g_ref, sh_ref, sc_ref, rw_ref, f_ref, aff_ref):
    y = _rms(x_ref[...], g_ref[...])
    f = y * (1.0 + sc_ref[0]) + sh_ref[0]
    fb = f.astype(BF16)
    f_ref[...] = fb
    logits = jnp.dot(fb, rw_ref[...].astype(BF16), preferred_element_type=F32)
    lane = lax.broadcasted_iota(jnp.int32, logits.shape, 1)
    logits = jnp.where(lane < N_EXPERTS, logits, MASK_VALUE)
    e = jnp.exp(logits - jnp.max(logits, axis=1, keepdims=True))
    aff_ref[...] = e / jnp.sum(e, axis=1, keepdims=True)


def norm_router(h, g, mod, shift_k, scale_k, router_w, row0):
    t0 = row0 // NORM_TILE
    n_tiles = (N_TOK - row0) // NORM_TILE
    rw = jnp.pad(router_w, ((0, 0), (0, ROUTER_LANES - N_EXPERTS)))

    def mod_map(k):
        return lambda i: (_segment(i + t0, NORM_TILE) * N_MOD + k, 0, 0)

    return pl.pallas_call(
        _norm_router_kernel,
        grid=(n_tiles,),
        in_specs=[pl.BlockSpec((NORM_TILE, D_MODEL), lambda i: (i + t0, 0)),
                  pl.BlockSpec((1, D_MODEL), lambda i: (0, 0)),
                  pl.BlockSpec((1, 1, D_MODEL), mod_map(shift_k)),
                  pl.BlockSpec((1, 1, D_MODEL), mod_map(scale_k)),
                  pl.BlockSpec((D_MODEL, ROUTER_LANES), lambda i: (0, 0))],
        out_specs=[pl.BlockSpec((NORM_TILE, D_MODEL), lambda i: (i, 0)),
                   pl.BlockSpec((NORM_TILE, ROUTER_LANES), lambda i: (i, 0))],
        out_shape=[jax.ShapeDtypeStruct((N_TOK - row0, D_MODEL), BF16),
                   jax.ShapeDtypeStruct((N_TOK - row0, ROUTER_LANES), F32)],
        compiler_params=_params(1),
        name="norm_router",
    )(h, g.reshape(1, D_MODEL), mod, mod, rw)


FF_CHUNK = 256
DOWN_TN = 512


def _ffn_up_kernel(x_ref, wg_ref, wu_ref, o_ref):
    x = x_ref[...]
    gate = jnp.dot(x, wg_ref[...].astype(BF16), preferred_element_type=F32)
    up = jnp.dot(x, wu_ref[...].astype(BF16), preferred_element_type=F32)
    o_ref[...] = (_silu(gate) * up).astype(o_ref.dtype)


def _ffn_down_kernel(h_ref, wd_ref, o_ref):
    o_ref[...] = jnp.dot(h_ref[...], wd_ref[...].astype(BF16), preferred_element_type=F32)


def expert_ffn(xin, w_gate, w_up, w_down, li):
    r = xin.shape[1]
    hid = pl.pallas_call(
        _ffn_up_kernel,
        grid=(N_EXPERTS, EXPERT_FF // FF_CHUNK),
        in_specs=[pl.BlockSpec((None, r, D_MODEL), lambda e, f: (e, 0, 0)),
                  pl.BlockSpec((None, None, D_MODEL, FF_CHUNK), lambda e, f: (li, e, 0, f)),
                  pl.BlockSpec((None, None, D_MODEL, FF_CHUNK), lambda e, f: (li, e, 0, f))],
        out_specs=pl.BlockSpec((None, r, FF_CHUNK), lambda e, f: (e, 0, f)),
        out_shape=jax.ShapeDtypeStruct((N_EXPERTS, r, EXPERT_FF), BF16),
        compiler_params=_params(2),
        name="ffn_up",
    )(xin, w_gate, w_up)
    return pl.pallas_call(
        _ffn_down_kernel,
        grid=(N_EXPERTS, D_MODEL // DOWN_TN),
        in_specs=[pl.BlockSpec((None, r, EXPERT_FF), lambda e, n: (e, 0, 0)),
                  pl.BlockSpec((None, None, EXPERT_FF, DOWN_TN), lambda e, n: (li, e, 0, n))],
        out_specs=pl.BlockSpec((None, r, DOWN_TN), lambda e, n: (e, 0, n)),
        out_shape=jax.ShapeDtypeStruct((N_EXPERTS, r, D_MODEL), F32),
        compiler_params=_params(2),
        name="ffn_down",
    )(hid, w_down)


COMBINE_CHUNK = 256

def _combine_kernel(rows_ref, y_ref, gate_ref, gf_ref, h_in_ref, h_ref, buf_ref, gsem, ssem, *, chunk, n_chunks):
    del h_in_ref
    e = pl.program_id(0)
    c = pl.program_id(1)
    n_e = pl.num_programs(0)
    slot = c % 2

    def row_copy(src_slot, j, base, gather):
        row = rows_ref[base + j]
        hbm = h_ref.at[pl.ds(row, 1)]
        vmem = buf_ref.at[src_slot, pl.ds(j, 1)]
        return (pltpu.make_async_copy(hbm, vmem, gsem.at[src_slot]) if gather
                else pltpu.make_async_copy(vmem, hbm, ssem.at[src_slot]))

    def start_all(s, ee, cc, gather):
        base = (ee * n_chunks + cc) * chunk

        def body(j, carry):
            row_copy(s, j, base, gather).start()
            return carry
        lax.fori_loop(0, chunk, body, 0)

    def wait_all(s, gather):
        def body(j, carry):
            row_copy(s, 0, 0, gather).wait()
            return carry
        lax.fori_loop(0, chunk, body, 0)

    @pl.when((e == 0) & (c == 0))
    def _():
        start_all(0, 0, 0, True)

    wait_all(slot, True)
    buf_ref[slot] = buf_ref[slot] + gf_ref[0] * (gate_ref[...] * y_ref[...])

    @pl.when(c > 0)
    def _():
        wait_all(1 - slot, False)

    @pl.when(c < n_chunks - 1)
    def _():
        start_all(1 - slot, e, c + 1, True)
        start_all(slot, e, c, False)

    @pl.when(c == n_chunks - 1)
    def _():
        start_all(slot, e, c, False)
        wait_all(slot, False)

        @pl.when(e < n_e - 1)
        def _():
            start_all(0, e + 1, 0, True)


def moe_combine(h, y, gates, rows, mod, gate_k, seg_of_chunk, chunk, row_off):
    n_e, r = rows.shape
    n_chunks = r // chunk
    blk0 = row_off // chunk
    d = h.shape[1]
    grid_spec = pltpu.PrefetchScalarGridSpec(
        num_scalar_prefetch=1,
        grid=(n_e, n_chunks),
        in_specs=[pl.BlockSpec((None, chunk, d), lambda e, c, rows: (e, blk0 + c, 0)),
                  pl.BlockSpec((None, chunk, 1), lambda e, c, rows: (e, c, 0)),
                  pl.BlockSpec((1, 1, d), lambda e, c, rows: (seg_of_chunk(c) * N_MOD + gate_k, 0, 0)),
                  pl.BlockSpec(memory_space=pl.ANY)],
        out_specs=pl.BlockSpec(memory_space=pl.ANY),
        scratch_shapes=[pltpu.VMEM((2, chunk, d), F32),
                        pltpu.SemaphoreType.DMA((2,)), pltpu.SemaphoreType.DMA((2,))])
    return pl.pallas_call(
        functools.partial(_combine_kernel, chunk=chunk, n_chunks=n_chunks),
        grid_spec=grid_spec,
        out_shape=jax.ShapeDtypeStruct(h.shape, h.dtype),
        input_output_aliases={4: 0},
        compiler_params=_params(2),
        name="moe_combine",
    )(rows.reshape(-1), y, gates.reshape(n_e, r, 1), mod, h)


def _final_norm_kernel(x_ref, g_ref, o_ref):
    o_ref[...] = _rms(x_ref[...], g_ref[...])


def final_norm(h, g):
    t0 = N_CTX // NORM_TILE
    return pl.pallas_call(
        _final_norm_kernel,
        grid=(N_LAT // NORM_TILE,),
        in_specs=[pl.BlockSpec((NORM_TILE, D_MODEL), lambda i: (i + t0, 0)),
                  pl.BlockSpec((1, D_MODEL), lambda i: (0, 0))],
        out_specs=pl.BlockSpec((NORM_TILE, D_MODEL), lambda i: (i, 0)),
        out_shape=jax.ShapeDtypeStruct((N_LAT, D_MODEL), F32),
        compiler_params=_params(1),
        name="final_norm",
    )(h, g.reshape(1, D_MODEL))


def _route(aff, n_per_sample, row0, aff_row0):
    cap = EC_CAPACITY * n_per_sample // N_EXPERTS
    lo = row0 - aff_row0
    a = aff[lo:lo + BATCH * n_per_sample, :N_EXPERTS].reshape(BATCH, n_per_sample, N_EXPERTS)
    gates, idx = lax.top_k(a.transpose(0, 2, 1), cap)
    rows = idx + (row0 + jnp.arange(BATCH) * n_per_sample)[:, None, None]
    return rows, gates


def _moe(h, f, aff, mod, big, li, update_ctx, f_row0):
    def per_expert(t):
        return t.transpose(1, 0, 2).reshape(N_EXPERTS, -1)

    rows, gates = _route(aff, SEQ, N_CTX, f_row0)
    chunks_per_sample = rows.shape[2] // COMBINE_CHUNK
    parts = [(per_expert(rows), per_expert(gates), lambda c: c // chunks_per_sample, COMBINE_CHUNK)]
    if update_ctx:
        rows_c, gates_c = _route(aff, CTX_LEN, 0, f_row0)
        parts.append((per_expert(rows_c), per_expert(gates_c), lambda c: MOD_CTX_ROW, BATCH * rows_c.shape[2]))
    all_rows = jnp.concatenate([part[0] for part in parts], axis=1)
    y = expert_ffn(f[all_rows - f_row0], big['w_gate'], big['w_up'], big['w_down'], li)
    off = 0
    for part_rows, part_gates, seg_of_chunk, chunk in parts:
        h = moe_combine(h, y, part_gates, part_rows, mod, 5, seg_of_chunk, chunk, off)
        off += part_rows.shape[1]
    return h


def _layer(h, c_rows, big, li, lp, cos_t, sin_t, update_ctx):
    row0 = 0 if update_ctx else N_CTX
    mod = adaln(c_rows, big['ada_w'], lp['ada_b'], li)[:3].reshape(3 * N_MOD, 1, D_MODEL)
    a = norm_mod(h, lp['norm1_g'], mod, 0, 1, 0)
    p = in_proj(a, big['w_in'], li)
    o_a = na_attention(p, lp['na_rpb'], with_ctx=update_ctx)
    o_f = retention_pass(p, lp['ret_log_decay'][0], cos_t, sin_t, backward=False)
    o_b = retention_pass(p, lp['ret_log_decay'][1], cos_t, sin_t, backward=True, o_fwd=o_f,
                         norm_g=lp['ret_norm_g'])
    o_c = spatial_gating(p, lp['sgu_ln_g'], lp['sgu_ln_b'], lp['sgu_w'], lp['sgu_b'], row0)
    h = out_proj_residual(o_a, o_b, o_c, big['w_out'], li, h, mod, 2, row0)
    f, aff = norm_router(h, lp['norm2_g'], mod, 3, 4, lp['router_w'], row0)
    return _moe(h, f, aff, mod, big, li, update_ctx, row0)


def kernel(x, c, ctx, c_ctx, ada_w, ada_b, norm1_g, norm2_g, w_in, w_out, na_rpb, ret_log_decay, ret_norm_g,
           sgu_ln_g, sgu_ln_b, sgu_w, sgu_b, router_w, w_gate, w_up, w_down, final_g):
    h = jnp.concatenate([ctx.reshape(N_CTX, D_MODEL), x.reshape(N_LAT, D_MODEL)], axis=0)
    c_rows = jnp.zeros((ADA_ROWS, D_MODEL), F32).at[:BATCH].set(c).at[MOD_CTX_ROW].set(c_ctx)
    cos_t, sin_t = _rope_tables()
    big = {'ada_w': ada_w, 'w_in': w_in, 'w_out': w_out, 'w_gate': w_gate, 'w_up': w_up, 'w_down': w_down}
    for i in range(DEPTH):
        lp = {
            'ada_b': ada_b[i], 'norm1_g': norm1_g[i], 'norm2_g': norm2_g[i],
            'na_rpb': na_rpb[i], 'ret_log_decay': ret_log_decay[i],
            'ret_norm_g': ret_norm_g[i], 'sgu_ln_g': sgu_ln_g[i], 'sgu_ln_b': sgu_ln_b[i],
            'sgu_w': sgu_w[i], 'sgu_b': sgu_b[i], 'router_w': router_w[i],
        }
        h = _layer(h, c_rows, big, i, lp, cos_t, sin_t, update_ctx=(i < DEPTH - 1))
    return final_norm(h, final_g).reshape(BATCH, SEQ, D_MODEL)
```

```python
import functools

import numpy as np
import jax
import jax.numpy as jnp
from jax import lax
from jax.experimental import pallas as pl
from jax.experimental.pallas import tpu as pltpu

D_MODEL = 4096
BATCH = 2
SEQ = 4096
DEPTH = 2
GRID_W = 64
GRID_ROWS = SEQ // GRID_W
CTX_LEN = 256
HEAD_DIM = 128
NA_HEADS = 12
RET_HEADS = 12
SGU_GROUPS = 8
SGU_CHUNK = 128
RET_CHUNK = 128
WIN_ROWS = 8
WIN_COLS = 16
NA_WIDTH = NA_HEADS * HEAD_DIM
RET_WIDTH = RET_HEADS * HEAD_DIM
SGU_WIDTH = SGU_GROUPS * HEAD_DIM
MIX_WIDTH = NA_WIDTH + RET_WIDTH + SGU_WIDTH
IN_WIDTH = 3 * NA_WIDTH + 4 * RET_WIDTH + 2 * SGU_WIDTH
N_EXPERTS = 16
EXPERT_FF = 1024
EC_CAPACITY = 2
N_MOD = 6
ROPE_BASE = 10000.0
NORM_EPS = 1e-6

N_CTX = BATCH * CTX_LEN
N_LAT = BATCH * SEQ
N_TOK = N_CTX + N_LAT
MOD_CTX_ROW = BATCH
ATT_SCALE = HEAD_DIM ** -0.5
MASK_VALUE = -1e30

Q_ROWS = 4
Q_BLK = Q_ROWS * GRID_W
K_BAND_ROWS = 12
K_BAND = K_BAND_ROWS * GRID_W
N_QBLK = GRID_ROWS // Q_ROWS

VMEM_LIMIT = 56 * 1024 * 1024

F32 = jnp.float32
BF16 = jnp.bfloat16


def _params(n_axes, vmem=VMEM_LIMIT):
    return pltpu.CompilerParams(dimension_semantics=("arbitrary",) * n_axes, vmem_limit_bytes=vmem)


def _segment(tile_idx, tile_rows):
    n_ctx_tiles = N_CTX // tile_rows
    return jnp.where(tile_idx < n_ctx_tiles, MOD_CTX_ROW, (tile_idx - n_ctx_tiles) // (SEQ // tile_rows))


def _silu(x):
    return x / (1.0 + jnp.exp(-x))


def _gelu_tanh(x):
    return x * (0.5 * (1.0 + jnp.tanh(np.float32(np.sqrt(2.0 / np.pi)) * (x + 0.044715 * (x * x * x)))))


ADA_TN = 512
ADA_ROWS = 16


def _adaln_kernel(c_ref, w_ref, b_ref, o_ref):
    s = _silu(c_ref[...]).astype(BF16)
    o_ref[...] = jnp.dot(s, w_ref[...].astype(BF16), preferred_element_type=F32) + b_ref[...]


def adaln(c_rows, ada_w, ada_b, li):
    n = ada_w.shape[2]
    return pl.pallas_call(
        _adaln_kernel,
        grid=(n // ADA_TN,),
        in_specs=[pl.BlockSpec((ADA_ROWS, D_MODEL), lambda j: (0, 0)),
                  pl.BlockSpec((None, D_MODEL, ADA_TN), lambda j: (li, 0, j)),
                  pl.BlockSpec((1, ADA_TN), lambda j: (0, j))],
        out_specs=pl.BlockSpec((ADA_ROWS, ADA_TN), lambda j: (0, j)),
        out_shape=jax.ShapeDtypeStruct((ADA_ROWS, n), F32),
        compiler_params=_params(1),
        name="adaln",
    )(c_rows, ada_w, ada_b.reshape(1, n))


NORM_TILE = 256


def _rms(x, g):
    return x * lax.rsqrt(jnp.mean(x * x, axis=-1, keepdims=True) + NORM_EPS) * g


def _norm_mod_kernel(x_ref, g_ref, sh_ref, sc_ref, o_ref):
    y = _rms(x_ref[...], g_ref[...])
    o_ref[...] = (y * (1.0 + sc_ref[0]) + sh_ref[0]).astype(o_ref.dtype)


def norm_mod(h, g, mod, shift_k, scale_k, row0):
    t0 = row0 // NORM_TILE
    n_tiles = (N_TOK - row0) // NORM_TILE

    def mod_map(k):
        return lambda i: (_segment(i + t0, NORM_TILE) * N_MOD + k, 0, 0)

    return pl.pallas_call(
        _norm_mod_kernel,
        grid=(n_tiles,),
        in_specs=[pl.BlockSpec((NORM_TILE, D_MODEL), lambda i: (i + t0, 0)),
                  pl.BlockSpec((1, D_MODEL), lambda i: (0, 0)),
                  pl.BlockSpec((1, 1, D_MODEL), mod_map(shift_k)),
                  pl.BlockSpec((1, 1, D_MODEL), mod_map(scale_k))],
        out_specs=pl.BlockSpec((NORM_TILE, D_MODEL), lambda i: (i + t0, 0)),
        out_shape=jax.ShapeDtypeStruct((N_TOK, D_MODEL), BF16),
        compiler_params=_params(1),
        name="norm_mod",
    )(h, g.reshape(1, D_MODEL), mod, mod)


PROJ_TM = 1088
PROJ_TN = 512


def _proj_kernel(a_ref, w_ref, o_ref, wbf_ref):
    @pl.when(pl.program_id(1) == 0)
    def _():
        wbf_ref[...] = w_ref[...].astype(BF16)

    o_ref[...] = jnp.dot(a_ref[...], wbf_ref[...], preferred_element_type=F32)


def in_proj(a, w_in, li):
    return pl.pallas_call(
        _proj_kernel,
        grid=(IN_WIDTH // PROJ_TN, N_TOK // PROJ_TM),
        in_specs=[pl.BlockSpec((PROJ_TM, D_MODEL), lambda n, m: (m, 0)),
                  pl.BlockSpec((None, D_MODEL, PROJ_TN), lambda n, m: (li, 0, n))],
        out_specs=pl.BlockSpec((PROJ_TM, PROJ_TN), lambda n, m: (m, n)),
        out_shape=jax.ShapeDtypeStruct((N_TOK, IN_WIDTH), F32),
        scratch_shapes=[pltpu.VMEM((D_MODEL, PROJ_TN), BF16)],
        compiler_params=_params(2),
        name="in_proj",
    )(a, w_in)


OUT_TM = 512
OUT_TN = 512


def _out_proj_kernel(oa_ref, ob_ref, oc_ref, w_ref, h_ref, gate_ref, o_ref, wbf_ref):
    @pl.when(pl.program_id(1) == 0)
    def _():
        wbf_ref[...] = w_ref[...].astype(BF16)

    y = jnp.dot(oa_ref[...], wbf_ref[0:NA_WIDTH], preferred_element_type=F32)
    y = y + jnp.dot(ob_ref[...], wbf_ref[NA_WIDTH:NA_WIDTH + RET_WIDTH], preferred_element_type=F32)
    y = y + jnp.dot(oc_ref[...], wbf_ref[NA_WIDTH + RET_WIDTH:MIX_WIDTH], preferred_element_type=F32)
    o_ref[...] = h_ref[...] + gate_ref[0] * y


def out_proj_residual(o_a, o_b, o_c, w_out, li, h, mod, gate_k, row0):
    t0 = row0 // OUT_TM
    n_tiles = (N_TOK - row0) // OUT_TM
    return pl.pallas_call(
        _out_proj_kernel,
        grid=(D_MODEL // OUT_TN, n_tiles),
        in_specs=[pl.BlockSpec((OUT_TM, NA_WIDTH), lambda n, m: (m, 0)),
                  pl.BlockSpec((OUT_TM, RET_WIDTH), lambda n, m: (m + t0, 0)),
                  pl.BlockSpec((OUT_TM, SGU_WIDTH), lambda n, m: (m, 0)),
                  pl.BlockSpec((None, MIX_WIDTH, OUT_TN), lambda n, m: (li, 0, n)),
                  pl.BlockSpec((OUT_TM, OUT_TN), lambda n, m: (m + t0, n)),
                  pl.BlockSpec((1, 1, OUT_TN), lambda n, m: (_segment(m + t0, OUT_TM) * N_MOD + gate_k, 0, n))],
        out_specs=pl.BlockSpec((OUT_TM, OUT_TN), lambda n, m: (m + t0, n)),
        out_shape=jax.ShapeDtypeStruct((N_TOK, D_MODEL), F32),
        scratch_shapes=[pltpu.VMEM((MIX_WIDTH, OUT_TN), BF16)],
        input_output_aliases={4: 0},
        compiler_params=_params(2),
        name="out_proj",
    )(o_a, o_b, o_c, w_out, h, mod)


def _na_kernel(q_ref, k0_ref, k1_ref, k2_ref, kc_ref, v0_ref, v1_ref, v2_ref, vc_ref, bias_ref, o_ref):
    for h in range(NA_HEADS):
        cols = slice(h * HEAD_DIM, (h + 1) * HEAD_DIM)
        q = (q_ref[:, cols] * ATT_SCALE).astype(BF16)

        def scores(k_ref):
            k = k_ref[:, cols].astype(BF16)
            return lax.dot_general(q, k, (((1,), (1,)), ((), ())), preferred_element_type=F32)

        s_win = jnp.concatenate([scores(k0_ref), scores(k1_ref), scores(k2_ref)], axis=1) + bias_ref[h]
        s_ctx = scores(kc_ref)
        m = jnp.maximum(jnp.max(s_win, axis=1, keepdims=True), jnp.max(s_ctx, axis=1, keepdims=True))
        e_win = jnp.exp(s_win - m)
        e_ctx = jnp.exp(s_ctx - m)
        denom = jnp.sum(e_win, axis=1, keepdims=True) + jnp.sum(e_ctx, axis=1, keepdims=True)
        acc = jnp.dot(e_ctx.astype(BF16), vc_ref[:, cols].astype(BF16), preferred_element_type=F32)
        for t, v_ref in enumerate((v0_ref, v1_ref, v2_ref)):
            e = e_win[:, t * Q_BLK:(t + 1) * Q_BLK].astype(BF16)
            acc = acc + jnp.dot(e, v_ref[:, cols].astype(BF16), preferred_element_type=F32)
        o_ref[:, cols] = (acc / denom).astype(o_ref.dtype)


def _na_bias_table(rpb):
    n_off = 2 * WIN_ROWS - 1
    pad = GRID_W - WIN_COLS
    ext = jnp.pad(rpb.astype(F32), ((0, 0), (0, 0), (pad, pad)))
    toep = jnp.stack([ext[..., GRID_W - 1 - qc:2 * GRID_W - 1 - qc] for qc in range(GRID_W)], axis=-2)
    qc = np.arange(GRID_W)
    c0 = np.clip(qc - WIN_COLS // 2, 0, GRID_W - WIN_COLS)
    ok_c = (qc[None, :] >= c0[:, None]) & (qc[None, :] < c0[:, None] + WIN_COLS)
    toep = jnp.where(ok_c, toep, MASK_VALUE)
    toep = jnp.concatenate([toep, jnp.full((NA_HEADS, 1, GRID_W, GRID_W), MASK_VALUE, F32)], axis=1)
    pats = []
    for blk in (0, 1, N_QBLK - 1):
        band0 = int(np.clip(Q_ROWS * blk - Q_ROWS, 0, GRID_ROWS - K_BAND_ROWS))
        rows = []
        for q in range(Q_ROWS):
            qr = Q_ROWS * blk + q
            r0 = int(np.clip(qr - WIN_ROWS // 2, 0, GRID_ROWS - WIN_ROWS))
            blocks = []
            for k in range(K_BAND_ROWS):
                kr = band0 + k
                off = kr - qr + (WIN_ROWS - 1) if r0 <= kr < r0 + WIN_ROWS else n_off
                blocks.append(toep[:, off])
            rows.append(jnp.concatenate(blocks, axis=-1))
        pats.append(jnp.concatenate(rows, axis=-2))
    pats.append(jnp.full((NA_HEADS, Q_BLK, K_BAND), MASK_VALUE, F32))
    return jnp.stack(pats, axis=0)


def na_attention(p, rpb, with_ctx):
    bias = _na_bias_table(rpb)
    ctx_blocks = N_CTX // Q_BLK
    lead = 1 if with_ctx else 0

    def lat_block(b, j):
        return ctx_blocks + b * N_QBLK + j

    def q_map(b, i):
        return (jnp.where(i < lead, b, lat_block(b, i - lead)), 0)

    def band_map(t, col):
        def f(b, i):
            return (lat_block(b, jnp.clip(i - lead - 1, 0, N_QBLK - 3)) + t, col)
        return f

    def ctx_map(col):
        return lambda b, i: (b, col)

    def bias_map(b, i):
        j = i - lead
        return (jnp.where(j < 0, 3, jnp.where(j == 0, 0, jnp.where(j == N_QBLK - 1, 2, 1))), 0, 0, 0)

    def out_map(b, i):
        if with_ctx:
            return (jnp.where(i < lead, b, lat_block(b, i - lead)), 0)
        return (b * N_QBLK + i, 0)

    blk = (Q_BLK, NA_WIDTH)
    return pl.pallas_call(
        _na_kernel,
        grid=(BATCH, N_QBLK + lead),
        in_specs=[pl.BlockSpec(blk, q_map),
                  pl.BlockSpec(blk, band_map(0, 1)), pl.BlockSpec(blk, band_map(1, 1)),
                  pl.BlockSpec(blk, band_map(2, 1)), pl.BlockSpec(blk, ctx_map(1)),
                  pl.BlockSpec(blk, band_map(0, 2)), pl.BlockSpec(blk, band_map(1, 2)),
                  pl.BlockSpec(blk, band_map(2, 2)), pl.BlockSpec(blk, ctx_map(2)),
                  pl.BlockSpec((None, NA_HEADS, Q_BLK, K_BAND), bias_map, pipeline_mode=pl.Buffered(1))],
        out_specs=pl.BlockSpec(blk, out_map),
        out_shape=jax.ShapeDtypeStruct((N_TOK if with_ctx else N_LAT, NA_WIDTH), BF16),
        compiler_params=_params(2),
        name="na_attention",
    )(p, p, p, p, p, p, p, p, p, bias)


CTX_CHUNKS = CTX_LEN // RET_CHUNK
LAT_CHUNKS = SEQ // RET_CHUNK
RET_STEPS = CTX_CHUNKS + LAT_CHUNKS
RET_Q_COL, RET_K_COL, RET_V_COL, RET_G_COL = 3, 4, 5, 6


def _ret_kernel(ld_ref, q_ref, k_ref, v_ref, cos_ref, sin_ref, *rest, backward):
    if backward:
        of_ref, g_ref, ng_ref, o_ref, st_ref = rest
    else:
        o_ref, st_ref = rest

    @pl.when(pl.program_id(1) == 0)
    def _():
        st_ref[...] = jnp.zeros_like(st_ref)

    ii = lax.broadcasted_iota(jnp.int32, (RET_CHUNK, RET_CHUNK), 0)
    jj = lax.broadcasted_iota(jnp.int32, (RET_CHUNK, RET_CHUNK), 1)
    diff = ((jj - ii) if backward else (ii - jj)).astype(F32)
    causal = diff >= 0.0
    diff = jnp.maximum(diff, 0.0)
    row = lax.broadcasted_iota(jnp.int32, (RET_CHUNK, 1), 0)
    pos = ((RET_CHUNK - 1 - row) if backward else row).astype(F32)
    first_half = (jj % (HEAD_DIM // 2)) < (HEAD_DIM // 4)
    cos = cos_ref[...]
    sin = sin_ref[...]

    def rope(x):
        partner = jnp.where(first_half, pltpu.roll(x, HEAD_DIM - HEAD_DIM // 4, axis=1),
                            pltpu.roll(x, HEAD_DIM // 4, axis=1))
        return x * cos + partner * sin

    for hh in range(RET_HEADS):
        cols = slice(hh * HEAD_DIM, (hh + 1) * HEAD_DIM)
        ld = ld_ref[hh]
        inner_decay = jnp.where(causal, jnp.exp(diff * ld), 0.0)
        q_decay = jnp.exp((pos + 1.0) * ld)
        k_decay = jnp.exp((RET_CHUNK - 1.0 - pos) * ld)
        chunk_decay = jnp.exp(jnp.full((1, HEAD_DIM), RET_CHUNK, F32) * ld)
        q = rope(q_ref[:, cols])
        k = rope(k_ref[:, cols]) * ATT_SCALE
        qb = q.astype(BF16)
        vb = v_ref[:, cols].astype(BF16)
        att = lax.dot_general(qb, k.astype(BF16), (((1,), (1,)), ((), ())), preferred_element_type=F32)
        inner = jnp.dot((att * inner_decay).astype(BF16), vb, preferred_element_type=F32)
        state = st_ref[hh]
        cross = jnp.dot(qb, state.astype(BF16), preferred_element_type=F32) * q_decay
        kd_t = (k * k_decay).T.astype(BF16)
        st_ref[hh] = state * chunk_decay + jnp.dot(kd_t, vb, preferred_element_type=F32)
        o = inner + cross
        if backward:
            tot = of_ref[:, cols] + o
            y = _rms(tot, ng_ref[:, cols]) * _silu(g_ref[:, cols])
            o_ref[:, cols] = y.astype(o_ref.dtype)
        else:
            o_ref[:, cols] = o


def _ret_chunk_block(b, s, backward):
    if backward:
        ctx_blk = b * CTX_CHUNKS + (CTX_CHUNKS - 1 - s)
        lat_blk = N_CTX // RET_CHUNK + b * LAT_CHUNKS + (LAT_CHUNKS - 1 - (s - CTX_CHUNKS))
    else:
        ctx_blk = b * CTX_CHUNKS + s
        lat_blk = N_CTX // RET_CHUNK + b * LAT_CHUNKS + (s - CTX_CHUNKS)
    return jnp.where(s < CTX_CHUNKS, ctx_blk, lat_blk)


def retention_pass(p, log_decay, cos_t, sin_t, backward, o_fwd=None, norm_g=None):
    def col_map(col):
        return lambda b, s: (_ret_chunk_block(b, s, backward), col)

    wide = (RET_CHUNK, RET_WIDTH)
    in_specs = [pl.BlockSpec(memory_space=pltpu.SMEM),
                pl.BlockSpec(wide, col_map(RET_Q_COL)), pl.BlockSpec(wide, col_map(RET_K_COL)),
                pl.BlockSpec(wide, col_map(RET_V_COL)),
                pl.BlockSpec((RET_CHUNK, HEAD_DIM), col_map(0)), pl.BlockSpec((RET_CHUNK, HEAD_DIM), col_map(0))]
    args = [log_decay, p, p, p, cos_t, sin_t]
    if backward:
        in_specs += [pl.BlockSpec(wide, col_map(0)), pl.BlockSpec(wide, col_map(RET_G_COL)),
                     pl.BlockSpec((1, RET_WIDTH), lambda b, s: (0, 0))]
        args += [o_fwd, p, norm_g.reshape(1, RET_WIDTH)]
    return pl.pallas_call(
        functools.partial(_ret_kernel, backward=backward),
        grid=(BATCH, RET_STEPS),
        in_specs=in_specs,
        out_specs=pl.BlockSpec(wide, col_map(0)),
        out_shape=jax.ShapeDtypeStruct((N_TOK, RET_WIDTH), BF16 if backward else F32),
        scratch_shapes=[pltpu.VMEM((RET_HEADS, HEAD_DIM, HEAD_DIM), F32)],
        compiler_params=_params(2),
        name="retention_bwd" if backward else "retention_fwd",
    )(*args)


def _rope_tables():
    quarter = HEAD_DIM // 4
    inv_freq = ROPE_BASE ** (-jnp.arange(quarter, dtype=F32) / quarter)
    t = jnp.arange(SEQ)
    ang_r = (t // GRID_W).astype(F32)[:, None] * inv_freq[None, :]
    ang_c = (t % GRID_W).astype(F32)[:, None] * inv_freq[None, :]
    cos = jnp.concatenate([jnp.cos(ang_r), jnp.cos(ang_r), jnp.cos(ang_c), jnp.cos(ang_c)], axis=-1)
    sin = jnp.concatenate([-jnp.sin(ang_r), jnp.sin(ang_r), -jnp.sin(ang_c), jnp.sin(ang_c)], axis=-1)
    cos = jnp.concatenate([jnp.ones((N_CTX, HEAD_DIM), F32)] + [cos] * BATCH, axis=0)
    sin = jnp.concatenate([jnp.zeros((N_CTX, HEAD_DIM), F32)] + [sin] * BATCH, axis=0)
    return cos, sin


SGU_TOK = 512
SGU_COLS = 512
SGU_U_COL = (3 * NA_WIDTH + 4 * RET_WIDTH) // SGU_COLS
SGU_V_COL = SGU_U_COL + SGU_WIDTH // SGU_COLS
SGU_GPB = SGU_COLS // HEAD_DIM


def _sgu_kernel(u_ref, v_ref, lng_ref, lnb_ref, w_ref, bs_ref, o_ref):
    for g in range(SGU_GPB):
        cols = slice(g * HEAD_DIM, (g + 1) * HEAD_DIM)
        w = w_ref[g].astype(BF16)
        lng = lng_ref[:, cols]
        lnb = lnb_ref[:, cols]
        bias = bs_ref[:, g:g + 1]
        for c in range(SGU_TOK // SGU_CHUNK):
            rows = slice(c * SGU_CHUNK, (c + 1) * SGU_CHUNK)
            v = _gelu_tanh(v_ref[rows, cols])
            mu = jnp.mean(v, axis=-1, keepdims=True)
            vc = v - mu
            var = jnp.mean(vc * vc, axis=-1, keepdims=True)
            vn = vc * lax.rsqrt(var + NORM_EPS) * lng + lnb
            s = jnp.dot(w, vn.astype(BF16), preferred_element_type=F32) + bias
            o_ref[rows, cols] = (_gelu_tanh(u_ref[rows, cols]) * s).astype(o_ref.dtype)


def spatial_gating(p, ln_g, ln_b, w_s, b_s, row0):
    t0 = row0 // SGU_TOK
    n_tiles = (N_TOK - row0) // SGU_TOK
    n_half = SGU_WIDTH // SGU_COLS
    bs_t = b_s.reshape(n_half, SGU_GPB, SGU_CHUNK).transpose(0, 2, 1)
    return pl.pallas_call(
        _sgu_kernel,
        grid=(n_tiles, n_half),
        in_specs=[pl.BlockSpec((SGU_TOK, SGU_COLS), lambda t, c: (t + t0, SGU_U_COL + c)),
                  pl.BlockSpec((SGU_TOK, SGU_COLS), lambda t, c: (t + t0, SGU_V_COL + c)),
                  pl.BlockSpec((1, SGU_COLS), lambda t, c: (0, c)),
                  pl.BlockSpec((1, SGU_COLS), lambda t, c: (0, c)),
                  pl.BlockSpec((SGU_GPB, SGU_CHUNK, SGU_CHUNK), lambda t, c: (c, 0, 0)),
                  pl.BlockSpec((None, SGU_CHUNK, SGU_GPB), lambda t, c: (c, 0, 0))],
        out_specs=pl.BlockSpec((SGU_TOK, SGU_COLS), lambda t, c: (t, c)),
        out_shape=jax.ShapeDtypeStruct((N_TOK - row0, SGU_WIDTH), BF16),
        compiler_params=_params(2),
        name="spatial_gating",
    )(p, p, ln_g.reshape(1, SGU_WIDTH), ln_b.reshape(1, SGU_WIDTH), w_s, bs_t)


ROUTER_LANES = 128


def _norm_router_kernel(x_ref, g_ref, sh_ref, sc_ref, rw_ref, f_ref, aff_ref):
    y = _rms(x_ref[...], g_ref[...])
    f = y * (1.0 + sc_ref[0]) + sh_ref[0]
    fb = f.astype(BF16)
    f_ref[...] = fb
    logits = jnp.dot(fb, rw_ref[...].astype(BF16), preferred_element_type=F32)
    lane = lax.broadcasted_iota(jnp.int32, logits.shape, 1)
    logits = jnp.where(lane < N_EXPERTS, logits, MASK_VALUE)
    e = jnp.exp(logits - jnp.max(logits, axis=1, keepdims=True))
    aff_ref[...] = e / jnp.sum(e, axis=1, keepdims=True)


def norm_router(h, g, mod, shift_k, scale_k, router_w, row0):
    t0 = row0 // NORM_TILE
    n_tiles = (N_TOK - row0) // NORM_TILE
    rw = jnp.pad(router_w, ((0, 0), (0, ROUTER_LANES - N_EXPERTS)))

    def mod_map(k):
        return lambda i: (_segment(i + t0, NORM_TILE) * N_MOD + k, 0, 0)

    return pl.pallas_call(
        _norm_router_kernel,
        grid=(n_tiles,),
        in_specs=[pl.BlockSpec((NORM_TILE, D_MODEL), lambda i: (i + t0, 0)),
                  pl.BlockSpec((1, D_MODEL), lambda i: (0, 0)),
                  pl.BlockSpec((1, 1, D_MODEL), mod_map(shift_k)),
                  pl.BlockSpec((1, 1, D_MODEL), mod_map(scale_k)),
                  pl.BlockSpec((D_MODEL, ROUTER_LANES), lambda i: (0, 0))],
        out_specs=[pl.BlockSpec((NORM_TILE, D_MODEL), lambda i: (i, 0)),
                   pl.BlockSpec((NORM_TILE, ROUTER_LANES), lambda i: (i, 0))],
        out_shape=[jax.ShapeDtypeStruct((N_TOK - row0, D_MODEL), BF16),
                   jax.ShapeDtypeStruct((N_TOK - row0, ROUTER_LANES), F32)],
        compiler_params=_params(1),
        name="norm_router",
    )(h, g.reshape(1, D_MODEL), mod, mod, rw)


FF_CHUNK = 256
DOWN_TN = 1024


def _ffn_up_kernel(x_ref, wg_ref, wu_ref, o_ref):
    x = x_ref[...]
    gate = jnp.dot(x, wg_ref[...].astype(BF16), preferred_element_type=F32)
    up = jnp.dot(x, wu_ref[...].astype(BF16), preferred_element_type=F32)
    o_ref[...] = (_silu(gate) * up).astype(o_ref.dtype)


def _ffn_down_kernel(h_ref, wd_ref, o_ref):
    o_ref[...] = jnp.dot(h_ref[...], wd_ref[...].astype(BF16), preferred_element_type=F32)


def expert_ffn(xin, w_gate, w_up, w_down, li):
    r = xin.shape[1]
    hid = pl.pallas_call(
        _ffn_up_kernel,
        grid=(N_EXPERTS, EXPERT_FF // FF_CHUNK),
        in_specs=[pl.BlockSpec((None, r, D_MODEL), lambda e, f: (e, 0, 0)),
                  pl.BlockSpec((None, None, D_MODEL, FF_CHUNK), lambda e, f: (li, e, 0, f)),
                  pl.BlockSpec((None, None, D_MODEL, FF_CHUNK), lambda e, f: (li, e, 0, f))],
        out_specs=pl.BlockSpec((None, r, FF_CHUNK), lambda e, f: (e, 0, f)),
        out_shape=jax.ShapeDtypeStruct((N_EXPERTS, r, EXPERT_FF), BF16),
        compiler_params=_params(2),
        name="ffn_up",
    )(xin, w_gate, w_up)
    return pl.pallas_call(
        _ffn_down_kernel,
        grid=(N_EXPERTS, D_MODEL // DOWN_TN),
        in_specs=[pl.BlockSpec((None, r, EXPERT_FF), lambda e, n: (e, 0, 0)),
                  pl.BlockSpec((None, None, EXPERT_FF, DOWN_TN), lambda e, n: (li, e, 0, n))],
        out_specs=pl.BlockSpec((None, r, DOWN_TN), lambda e, n: (e, 0, n)),
        out_shape=jax.ShapeDtypeStruct((N_EXPERTS, r, D_MODEL), F32),
        compiler_params=_params(2),
        name="ffn_down",
    )(hid, w_down)


COMBINE_CHUNK = 256
COMBINE_UNROLL = 8

def _combine_kernel(rows_ref, y_ref, gate_ref, gf_ref, h_in_ref, h_ref, buf_ref, gsem, ssem, *, chunk, n_chunks):
    del h_in_ref
    e = pl.program_id(0)
    c = pl.program_id(1)
    n_e = pl.num_programs(0)
    slot = c % 2

    def copy(hbm, vmem, s, gather):
        return (pltpu.make_async_copy(hbm, vmem, gsem.at[s]) if gather
                else pltpu.make_async_copy(vmem, hbm, ssem.at[s]))

    def start_all(s, ee, cc, gather):
        base = (ee * n_chunks + cc) * chunk

        def body(g, carry):
            for u in range(COMBINE_UNROLL):
                j = g * COMBINE_UNROLL + u
                row = rows_ref[base + j]
                copy(h_ref.at[pl.ds(row, 1)], buf_ref.at[s, pl.ds(j, 1)], s, gather).start(priority=u % 2)
            return carry
        lax.fori_loop(0, chunk // COMBINE_UNROLL, body, 0)

    def wait_all(s, gather):
        copy(h_ref.at[pl.ds(0, chunk)], buf_ref.at[s], s, gather).wait()

    @pl.when((e == 0) & (c == 0))
    def _():
        start_all(0, 0, 0, True)

    wait_all(slot, True)
    buf_ref[slot] = buf_ref[slot] + gf_ref[0] * (gate_ref[...] * y_ref[...])

    @pl.when(c > 0)
    def _():
        wait_all(1 - slot, False)

    @pl.when(c < n_chunks - 1)
    def _():
        start_all(1 - slot, e, c + 1, True)
        start_all(slot, e, c, False)

    @pl.when(c == n_chunks - 1)
    def _():
        start_all(slot, e, c, False)
        wait_all(slot, False)

        @pl.when(e < n_e - 1)
        def _():
            start_all(0, e + 1, 0, True)


def moe_combine(h, y, gates, rows, mod, gate_k, seg_of_chunk, chunk, row_off):
    n_e, r = rows.shape
    n_chunks = r // chunk
    blk0 = row_off // chunk
    d = h.shape[1]
    grid_spec = pltpu.PrefetchScalarGridSpec(
        num_scalar_prefetch=1,
        grid=(n_e, n_chunks),
        in_specs=[pl.BlockSpec((None, chunk, d), lambda e, c, rows: (e, blk0 + c, 0)),
                  pl.BlockSpec((None, chunk, 1), lambda e, c, rows: (e, c, 0)),
                  pl.BlockSpec((1, 1, d), lambda e, c, rows: (seg_of_chunk(c) * N_MOD + gate_k, 0, 0)),
                  pl.BlockSpec(memory_space=pl.ANY)],
        out_specs=pl.BlockSpec(memory_space=pl.ANY),
        scratch_shapes=[pltpu.VMEM((2, chunk, d), F32),
                        pltpu.SemaphoreType.DMA((2,)), pltpu.SemaphoreType.DMA((2,))])
    return pl.pallas_call(
        functools.partial(_combine_kernel, chunk=chunk, n_chunks=n_chunks),
        grid_spec=grid_spec,
        out_shape=jax.ShapeDtypeStruct(h.shape, h.dtype),
        input_output_aliases={4: 0},
        compiler_params=_params(2),
        name="moe_combine",
    )(rows.reshape(-1), y, gates.reshape(n_e, r, 1), mod, h)


def _final_norm_kernel(x_ref, g_ref, o_ref):
    o_ref[...] = _rms(x_ref[...], g_ref[...])


def final_norm(h, g):
    t0 = N_CTX // NORM_TILE
    return pl.pallas_call(
        _final_norm_kernel,
        grid=(N_LAT // NORM_TILE,),
        in_specs=[pl.BlockSpec((NORM_TILE, D_MODEL), lambda i: (i + t0, 0)),
                  pl.BlockSpec((1, D_MODEL), lambda i: (0, 0))],
        out_specs=pl.BlockSpec((NORM_TILE, D_MODEL), lambda i: (i, 0)),
        out_shape=jax.ShapeDtypeStruct((N_LAT, D_MODEL), F32),
        compiler_params=_params(1),
        name="final_norm",
    )(h, g.reshape(1, D_MODEL))


def _route(aff, n_per_sample, row0, aff_row0):
    cap = EC_CAPACITY * n_per_sample // N_EXPERTS
    lo = row0 - aff_row0
    a = aff[lo:lo + BATCH * n_per_sample, :N_EXPERTS].reshape(BATCH, n_per_sample, N_EXPERTS)
    gates, idx = lax.top_k(a.transpose(0, 2, 1), cap)
    rows = idx + (row0 + jnp.arange(BATCH) * n_per_sample)[:, None, None]
    return rows, gates


def _moe(h, f, aff, mod, big, li, update_ctx, f_row0):
    def per_expert(t):
        return t.transpose(1, 0, 2).reshape(N_EXPERTS, -1)

    rows, gates = _route(aff, SEQ, N_CTX, f_row0)
    chunks_per_sample = rows.shape[2] // COMBINE_CHUNK
    parts = [(per_expert(rows), per_expert(gates), lambda c: c // chunks_per_sample, COMBINE_CHUNK)]
    if update_ctx:
        rows_c, gates_c = _route(aff, CTX_LEN, 0, f_row0)
        parts.append((per_expert(rows_c), per_expert(gates_c), lambda c: MOD_CTX_ROW, BATCH * rows_c.shape[2]))
    all_rows = jnp.concatenate([part[0] for part in parts], axis=1)
    y = expert_ffn(f[all_rows - f_row0], big['w_gate'], big['w_up'], big['w_down'], li)
    off = 0
    for part_rows, part_gates, seg_of_chunk, chunk in parts:
        h = moe_combine(h, y, part_gates, part_rows, mod, 5, seg_of_chunk, chunk, off)
        off += part_rows.shape[1]
    return h


def _layer(h, c_rows, big, li, lp, cos_t, sin_t, update_ctx):
    row0 = 0 if update_ctx else N_CTX
    mod = adaln(c_rows, big['ada_w'], lp['ada_b'], li)[:3].reshape(3 * N_MOD, 1, D_MODEL)
    a = norm_mod(h, lp['norm1_g'], mod, 0, 1, 0)
    p = in_proj(a, big['w_in'], li)
    o_a = na_attention(p, lp['na_rpb'], with_ctx=update_ctx)
    o_f = retention_pass(p, lp['ret_log_decay'][0], cos_t, sin_t, backward=False)
    o_b = retention_pass(p, lp['ret_log_decay'][1], cos_t, sin_t, backward=True, o_fwd=o_f,
                         norm_g=lp['ret_norm_g'])
    o_c = spatial_gating(p, lp['sgu_ln_g'], lp['sgu_ln_b'], lp['sgu_w'], lp['sgu_b'], row0)
    h = out_proj_residual(o_a, o_b, o_c, big['w_out'], li, h, mod, 2, row0)
    f, aff = norm_router(h, lp['norm2_g'], mod, 3, 4, lp['router_w'], row0)
    return _moe(h, f, aff, mod, big, li, update_ctx, row0)


def kernel(x, c, ctx, c_ctx, ada_w, ada_b, norm1_g, norm2_g, w_in, w_out, na_rpb, ret_log_decay, ret_norm_g,
           sgu_ln_g, sgu_ln_b, sgu_w, sgu_b, router_w, w_gate, w_up, w_down, final_g):
    h = jnp.concatenate([ctx.reshape(N_CTX, D_MODEL), x.reshape(N_LAT, D_MODEL)], axis=0)
    c_rows = jnp.zeros((ADA_ROWS, D_MODEL), F32).at[:BATCH].set(c).at[MOD_CTX_ROW].set(c_ctx)
    cos_t, sin_t = _rope_tables()
    big = {'ada_w': ada_w, 'w_in': w_in, 'w_out': w_out, 'w_gate': w_gate, 'w_up': w_up, 'w_down': w_down}
    for i in range(DEPTH):
        lp = {
            'ada_b': ada_b[i], 'norm1_g': norm1_g[i], 'norm2_g': norm2_g[i],
            'na_rpb': na_rpb[i], 'ret_log_decay': ret_log_decay[i],
            'ret_norm_g': ret_norm_g[i], 'sgu_ln_g': sgu_ln_g[i], 'sgu_ln_b': sgu_ln_b[i],
            'sgu_w': sgu_w[i], 'sgu_b': sgu_b[i], 'router_w': router_w[i],
        }
        h = _layer(h, c_rows, big, i, lp, cos_t, sin_t, update_ctx=(i < DEPTH - 1))
    return final_norm(h, final_g).reshape(BATCH, SEQ, D_MODEL)
```

```python
import functools

import numpy as np
import jax
import jax.numpy as jnp
from jax import lax
from jax.experimental import pallas as pl
from jax.experimental.pallas import tpu as pltpu

D_MODEL = 4096
BATCH = 2
SEQ = 4096
DEPTH = 2
GRID_W = 64
GRID_ROWS = SEQ // GRID_W
CTX_LEN = 256
HEAD_DIM = 128
NA_HEADS = 12
RET_HEADS = 12
SGU_GROUPS = 8
SGU_CHUNK = 128
RET_CHUNK = 128
WIN_ROWS = 8
WIN_COLS = 16
NA_WIDTH = NA_HEADS * HEAD_DIM
RET_WIDTH = RET_HEADS * HEAD_DIM
SGU_WIDTH = SGU_GROUPS * HEAD_DIM
MIX_WIDTH = NA_WIDTH + RET_WIDTH + SGU_WIDTH
IN_WIDTH = 3 * NA_WIDTH + 4 * RET_WIDTH + 2 * SGU_WIDTH
N_EXPERTS = 16
EXPERT_FF = 1024
EC_CAPACITY = 2
N_MOD = 6
ROPE_BASE = 10000.0
NORM_EPS = 1e-6

N_CTX = BATCH * CTX_LEN
N_LAT = BATCH * SEQ
N_TOK = N_CTX + N_LAT
MOD_CTX_ROW = BATCH
ATT_SCALE = HEAD_DIM ** -0.5
MASK_VALUE = -1e30

Q_ROWS = 4
Q_BLK = Q_ROWS * GRID_W
K_BAND_ROWS = 12
K_BAND = K_BAND_ROWS * GRID_W
N_QBLK = GRID_ROWS // Q_ROWS

VMEM_LIMIT = 56 * 1024 * 1024

F32 = jnp.float32
BF16 = jnp.bfloat16


def _params(n_axes, vmem=VMEM_LIMIT):
    return pltpu.CompilerParams(dimension_semantics=("arbitrary",) * n_axes, vmem_limit_bytes=vmem)


def _segment(tile_idx, tile_rows):
    return jnp.where(tile_idx < N_LAT // tile_rows, tile_idx // (SEQ // tile_rows), MOD_CTX_ROW)


def _silu(x):
    return x / (1.0 + jnp.exp(-x))


def _gelu_tanh(x):
    return x * (0.5 * (1.0 + jnp.tanh(np.float32(np.sqrt(2.0 / np.pi)) * (x + 0.044715 * (x * x * x)))))


ADA_TN = 512
ADA_ROWS = 16


def _adaln_kernel(c_ref, w_ref, b_ref, o_ref):
    s = _silu(c_ref[...]).astype(BF16)
    o_ref[...] = jnp.dot(s, w_ref[...].astype(BF16), preferred_element_type=F32) + b_ref[...]


def adaln(c_rows, ada_w, ada_b, li):
    n = ada_w.shape[2]
    return pl.pallas_call(
        _adaln_kernel,
        grid=(n // ADA_TN,),
        in_specs=[pl.BlockSpec((ADA_ROWS, D_MODEL), lambda j: (0, 0)),
                  pl.BlockSpec((None, D_MODEL, ADA_TN), lambda j: (li, 0, j)),
                  pl.BlockSpec((1, ADA_TN), lambda j: (0, j))],
        out_specs=pl.BlockSpec((ADA_ROWS, ADA_TN), lambda j: (0, j)),
        out_shape=jax.ShapeDtypeStruct((ADA_ROWS, n), F32),
        compiler_params=_params(1),
        name="adaln",
    )(c_rows, ada_w, ada_b.reshape(1, n))


NORM_TILE = 256


def _rms(x, g):
    return x * lax.rsqrt(jnp.mean(x * x, axis=-1, keepdims=True) + NORM_EPS) * g


def _norm_mod_kernel(x_ref, g_ref, sh_ref, sc_ref, o_ref):
    y = _rms(x_ref[...], g_ref[...])
    o_ref[...] = (y * (1.0 + sc_ref[0]) + sh_ref[0]).astype(o_ref.dtype)


def norm_mod(h, g, mod, shift_k, scale_k):
    def mod_map(k):
        return lambda i: (_segment(i, NORM_TILE) * N_MOD + k, 0, 0)

    return pl.pallas_call(
        _norm_mod_kernel,
        grid=(N_TOK // NORM_TILE,),
        in_specs=[pl.BlockSpec((NORM_TILE, D_MODEL), lambda i: (i, 0)),
                  pl.BlockSpec((1, D_MODEL), lambda i: (0, 0)),
                  pl.BlockSpec((1, 1, D_MODEL), mod_map(shift_k)),
                  pl.BlockSpec((1, 1, D_MODEL), mod_map(scale_k))],
        out_specs=pl.BlockSpec((NORM_TILE, D_MODEL), lambda i: (i, 0)),
        out_shape=jax.ShapeDtypeStruct((N_TOK, D_MODEL), BF16),
        compiler_params=_params(1),
        name="norm_mod",
    )(h, g.reshape(1, D_MODEL), mod, mod)


PROJ_TM = 2176
PROJ_TN = 256


def _proj_kernel(a_ref, w_ref, o_ref):
    o_ref[...] = jnp.dot(a_ref[...], w_ref[...].astype(BF16), preferred_element_type=F32)


def in_proj(a, w_in, li):
    return pl.pallas_call(
        _proj_kernel,
        grid=(N_TOK // PROJ_TM, IN_WIDTH // PROJ_TN),
        in_specs=[pl.BlockSpec((PROJ_TM, D_MODEL), lambda m, n: (m, 0), pipeline_mode=pl.Buffered(1)),
                  pl.BlockSpec((None, D_MODEL, PROJ_TN), lambda m, n: (li, 0, n))],
        out_specs=pl.BlockSpec((PROJ_TM, PROJ_TN), lambda m, n: (m, n)),
        out_shape=jax.ShapeDtypeStruct((N_TOK, IN_WIDTH), F32),
        compiler_params=_params(2),
        name="in_proj",
    )(a, w_in)


OUT_TN = 256


def _out_proj_kernel(oa_ref, ob_ref, oc_ref, w_ref, h_ref, g0_ref, g1_ref, gc_ref, o_ref):
    w = w_ref[...].astype(BF16)
    y = jnp.dot(oa_ref[...], w[0:NA_WIDTH], preferred_element_type=F32)
    y = y + jnp.dot(ob_ref[...], w[NA_WIDTH:NA_WIDTH + RET_WIDTH], preferred_element_type=F32)
    y = y + jnp.dot(oc_ref[...], w[NA_WIDTH + RET_WIDTH:MIX_WIDTH], preferred_element_type=F32)
    tm = o_ref.shape[0]
    row = pl.program_id(0) * tm + lax.broadcasted_iota(jnp.int32, (tm, 1), 0)
    gate = jnp.where(row < SEQ, g0_ref[0], jnp.where(row < N_LAT, g1_ref[0], gc_ref[0]))
    o_ref[...] = h_ref[...] + gate * y


def out_proj_residual(o_a, o_b, o_c, w_out, li, h, mod, gate_k, n_rows):
    tm = n_rows // 4

    def gate_map(seg):
        return lambda m, n: (seg * N_MOD + gate_k, 0, n)

    def row_tile(width):
        return pl.BlockSpec((tm, width), lambda m, n: (m, 0), pipeline_mode=pl.Buffered(1))

    gate_spec = [pl.BlockSpec((1, 1, OUT_TN), gate_map(seg)) for seg in (0, 1, MOD_CTX_ROW)]
    return pl.pallas_call(
        _out_proj_kernel,
        grid=(n_rows // tm, D_MODEL // OUT_TN),
        in_specs=[row_tile(NA_WIDTH), row_tile(RET_WIDTH), row_tile(SGU_WIDTH),
                  pl.BlockSpec((None, MIX_WIDTH, OUT_TN), lambda m, n: (li, 0, n)),
                  pl.BlockSpec((tm, OUT_TN), lambda m, n: (m, n))] + gate_spec,
        out_specs=pl.BlockSpec((tm, OUT_TN), lambda m, n: (m, n)),
        out_shape=jax.ShapeDtypeStruct((N_TOK, D_MODEL), F32),
        input_output_aliases={4: 0},
        compiler_params=_params(2),
        name="out_proj",
    )(o_a, o_b, o_c, w_out, h, mod, mod, mod)


def _na_kernel(q_ref, k0_ref, k1_ref, k2_ref, kc_ref, v0_ref, v1_ref, v2_ref, vc_ref, bias_ref, o_ref):
    for h in range(NA_HEADS):
        cols = slice(h * HEAD_DIM, (h + 1) * HEAD_DIM)
        q = (q_ref[:, cols] * ATT_SCALE).astype(BF16)

        def scores(k_ref):
            k = k_ref[:, cols].astype(BF16)
            return lax.dot_general(q, k, (((1,), (1,)), ((), ())), preferred_element_type=F32)

        s_win = jnp.concatenate([scores(k0_ref), scores(k1_ref), scores(k2_ref)], axis=1) + bias_ref[h]
        s_ctx = scores(kc_ref)
        m = jnp.maximum(jnp.max(s_win, axis=1, keepdims=True), jnp.max(s_ctx, axis=1, keepdims=True))
        e_win = jnp.exp(s_win - m)
        e_ctx = jnp.exp(s_ctx - m)
        denom = jnp.sum(e_win, axis=1, keepdims=True) + jnp.sum(e_ctx, axis=1, keepdims=True)
        acc = jnp.dot(e_ctx.astype(BF16), vc_ref[:, cols].astype(BF16), preferred_element_type=F32)
        for t, v_ref in enumerate((v0_ref, v1_ref, v2_ref)):
            e = e_win[:, t * Q_BLK:(t + 1) * Q_BLK].astype(BF16)
            acc = acc + jnp.dot(e, v_ref[:, cols].astype(BF16), preferred_element_type=F32)
        o_ref[:, cols] = (acc / denom).astype(o_ref.dtype)


def _na_bias_table(rpb):
    n_off = 2 * WIN_ROWS - 1
    pad = GRID_W - WIN_COLS
    ext = jnp.pad(rpb.astype(F32), ((0, 0), (0, 0), (pad, pad)))
    toep = jnp.stack([ext[..., GRID_W - 1 - qc:2 * GRID_W - 1 - qc] for qc in range(GRID_W)], axis=-2)
    qc = np.arange(GRID_W)
    c0 = np.clip(qc - WIN_COLS // 2, 0, GRID_W - WIN_COLS)
    ok_c = (qc[None, :] >= c0[:, None]) & (qc[None, :] < c0[:, None] + WIN_COLS)
    toep = jnp.where(ok_c, toep, MASK_VALUE)
    toep = jnp.concatenate([toep, jnp.full((NA_HEADS, 1, GRID_W, GRID_W), MASK_VALUE, F32)], axis=1)
    pats = []
    for blk in (0, 1, N_QBLK - 1):
        band0 = int(np.clip(Q_ROWS * blk - Q_ROWS, 0, GRID_ROWS - K_BAND_ROWS))
        rows = []
        for q in range(Q_ROWS):
            qr = Q_ROWS * blk + q
            r0 = int(np.clip(qr - WIN_ROWS // 2, 0, GRID_ROWS - WIN_ROWS))
            blocks = []
            for k in range(K_BAND_ROWS):
                kr = band0 + k
                off = kr - qr + (WIN_ROWS - 1) if r0 <= kr < r0 + WIN_ROWS else n_off
                blocks.append(toep[:, off])
            rows.append(jnp.concatenate(blocks, axis=-1))
        pats.append(jnp.concatenate(rows, axis=-2))
    pats.append(jnp.full((NA_HEADS, Q_BLK, K_BAND), MASK_VALUE, F32))
    return jnp.stack(pats, axis=0)


def na_attention(p, rpb, with_ctx):
    bias = _na_bias_table(rpb)
    lead = 1 if with_ctx else 0

    def lat_block(b, j):
        return b * N_QBLK + j

    def ctx_block(b):
        return N_LAT // Q_BLK + b

    def q_map(b, i):
        return (jnp.where(i < lead, ctx_block(b), lat_block(b, i - lead)), 0)

    def band_map(t, col):
        def f(b, i):
            return (lat_block(b, jnp.clip(i - lead - 1, 0, N_QBLK - 3)) + t, col)
        return f

    def ctx_map(col):
        return lambda b, i: (ctx_block(b), col)

    def bias_map(b, i):
        j = i - lead
        return (jnp.where(j < 0, 3, jnp.where(j == 0, 0, jnp.where(j == N_QBLK - 1, 2, 1))), 0, 0, 0)

    def out_map(b, i):
        return (jnp.where(i < lead, ctx_block(b), lat_block(b, i - lead)), 0)

    blk = (Q_BLK, NA_WIDTH)
    return pl.pallas_call(
        _na_kernel,
        grid=(BATCH, N_QBLK + lead),
        in_specs=[pl.BlockSpec(blk, q_map),
                  pl.BlockSpec(blk, band_map(0, 1)), pl.BlockSpec(blk, band_map(1, 1)),
                  pl.BlockSpec(blk, band_map(2, 1)), pl.BlockSpec(blk, ctx_map(1)),
                  pl.BlockSpec(blk, band_map(0, 2)), pl.BlockSpec(blk, band_map(1, 2)),
                  pl.BlockSpec(blk, band_map(2, 2)), pl.BlockSpec(blk, ctx_map(2)),
                  pl.BlockSpec((None, NA_HEADS, Q_BLK, K_BAND), bias_map, pipeline_mode=pl.Buffered(1))],
        out_specs=pl.BlockSpec(blk, out_map),
        out_shape=jax.ShapeDtypeStruct((N_TOK if with_ctx else N_LAT, NA_WIDTH), BF16),
        compiler_params=_params(2),
        name="na_attention",
    )(p, p, p, p, p, p, p, p, p, bias)


CTX_CHUNKS = CTX_LEN // RET_CHUNK
LAT_CHUNKS = SEQ // RET_CHUNK
RET_STEPS = CTX_CHUNKS + LAT_CHUNKS
RET_Q_COL, RET_K_COL, RET_V_COL, RET_G_COL = 3, 4, 5, 6


def _ret_kernel(ld_ref, q_ref, k_ref, v_ref, cos_ref, sin_ref, *rest, backward):
    if backward:
        of_ref, g_ref, ng_ref, o_ref, st_ref = rest
    else:
        o_ref, st_ref = rest

    @pl.when(pl.program_id(1) == 0)
    def _():
        st_ref[...] = jnp.zeros_like(st_ref)

    ii = lax.broadcasted_iota(jnp.int32, (RET_CHUNK, RET_CHUNK), 0)
    jj = lax.broadcasted_iota(jnp.int32, (RET_CHUNK, RET_CHUNK), 1)
    diff = ((jj - ii) if backward else (ii - jj)).astype(F32)
    causal = diff >= 0.0
    diff = jnp.maximum(diff, 0.0)
    row = lax.broadcasted_iota(jnp.int32, (RET_CHUNK, 1), 0)
    pos = ((RET_CHUNK - 1 - row) if backward else row).astype(F32)
    first_half = (jj % (HEAD_DIM // 2)) < (HEAD_DIM // 4)
    cos = cos_ref[...]
    sin = sin_ref[...]

    def rope(x):
        partner = jnp.where(first_half, pltpu.roll(x, HEAD_DIM - HEAD_DIM // 4, axis=1),
                            pltpu.roll(x, HEAD_DIM // 4, axis=1))
        return x * cos + partner * sin

    for hh in range(RET_HEADS):
        cols = slice(hh * HEAD_DIM, (hh + 1) * HEAD_DIM)
        ld = ld_ref[hh]
        inner_decay = jnp.where(causal, jnp.exp(diff * ld), 0.0)
        q_decay = jnp.exp((pos + 1.0) * ld)
        k_decay = jnp.exp((RET_CHUNK - 1.0 - pos) * ld)
        chunk_decay = jnp.exp(jnp.full((1, HEAD_DIM), RET_CHUNK, F32) * ld)
        q = rope(q_ref[:, cols])
        k = rope(k_ref[:, cols]) * ATT_SCALE
        qb = q.astype(BF16)
        vb = v_ref[:, cols].astype(BF16)
        att = lax.dot_general(qb, k.astype(BF16), (((1,), (1,)), ((), ())), preferred_element_type=F32)
        inner = jnp.dot((att * inner_decay).astype(BF16), vb, preferred_element_type=F32)
        state = st_ref[hh]
        cross = jnp.dot(qb, state.astype(BF16), preferred_element_type=F32) * q_decay
        kd_t = (k * k_decay).T.astype(BF16)
        st_ref[hh] = state * chunk_decay + jnp.dot(kd_t, vb, preferred_element_type=F32)
        o = inner + cross
        if backward:
            tot = of_ref[:, cols] + o
            y = _rms(tot, ng_ref[:, cols]) * _silu(g_ref[:, cols])
            o_ref[:, cols] = y.astype(o_ref.dtype)
        else:
            o_ref[:, cols] = o


def _ret_chunk_block(b, s, backward):
    ctx0 = N_LAT // RET_CHUNK + b * CTX_CHUNKS
    lat0 = b * LAT_CHUNKS
    if backward:
        ctx_blk = ctx0 + (CTX_CHUNKS - 1 - s)
        lat_blk = lat0 + (LAT_CHUNKS - 1 - (s - CTX_CHUNKS))
    else:
        ctx_blk = ctx0 + s
        lat_blk = lat0 + (s - CTX_CHUNKS)
    return jnp.where(s < CTX_CHUNKS, ctx_blk, lat_blk)


def retention_pass(p, log_decay, cos_t, sin_t, backward, o_fwd=None, norm_g=None):
    def col_map(col):
        return lambda b, s: (_ret_chunk_block(b, s, backward), col)

    wide = (RET_CHUNK, RET_WIDTH)
    in_specs = [pl.BlockSpec(memory_space=pltpu.SMEM),
                pl.BlockSpec(wide, col_map(RET_Q_COL)), pl.BlockSpec(wide, col_map(RET_K_COL)),
                pl.BlockSpec(wide, col_map(RET_V_COL)),
                pl.BlockSpec((RET_CHUNK, HEAD_DIM), col_map(0)), pl.BlockSpec((RET_CHUNK, HEAD_DIM), col_map(0))]
    args = [log_decay, p, p, p, cos_t, sin_t]
    if backward:
        in_specs += [pl.BlockSpec(wide, col_map(0)), pl.BlockSpec(wide, col_map(RET_G_COL)),
                     pl.BlockSpec((1, RET_WIDTH), lambda b, s: (0, 0))]
        args += [o_fwd, p, norm_g.reshape(1, RET_WIDTH)]
    return pl.pallas_call(
        functools.partial(_ret_kernel, backward=backward),
        grid=(BATCH, RET_STEPS),
        in_specs=in_specs,
        out_specs=pl.BlockSpec(wide, col_map(0)),
        out_shape=jax.ShapeDtypeStruct((N_TOK, RET_WIDTH), BF16 if backward else F32),
        scratch_shapes=[pltpu.VMEM((RET_HEADS, HEAD_DIM, HEAD_DIM), F32)],
        compiler_params=_params(2),
        name="retention_bwd" if backward else "retention_fwd",
    )(*args)


def _rope_tables():
    quarter = HEAD_DIM // 4
    inv_freq = ROPE_BASE ** (-jnp.arange(quarter, dtype=F32) / quarter)
    t = jnp.arange(SEQ)
    ang_r = (t // GRID_W).astype(F32)[:, None] * inv_freq[None, :]
    ang_c = (t % GRID_W).astype(F32)[:, None] * inv_freq[None, :]
    cos = jnp.concatenate([jnp.cos(ang_r), jnp.cos(ang_r), jnp.cos(ang_c), jnp.cos(ang_c)], axis=-1)
    sin = jnp.concatenate([-jnp.sin(ang_r), jnp.sin(ang_r), -jnp.sin(ang_c), jnp.sin(ang_c)], axis=-1)
    cos = jnp.concatenate([cos] * BATCH + [jnp.ones((N_CTX, HEAD_DIM), F32)], axis=0)
    sin = jnp.concatenate([sin] * BATCH + [jnp.zeros((N_CTX, HEAD_DIM), F32)], axis=0)
    return cos, sin


SGU_TOK = 512
SGU_COLS = 512
SGU_U_COL = (3 * NA_WIDTH + 4 * RET_WIDTH) // SGU_COLS
SGU_V_COL = SGU_U_COL + SGU_WIDTH // SGU_COLS
SGU_GPB = SGU_COLS // HEAD_DIM


def _sgu_kernel(u_ref, v_ref, lng_ref, lnb_ref, w_ref, bs_ref, o_ref):
    for g in range(SGU_GPB):
        cols = slice(g * HEAD_DIM, (g + 1) * HEAD_DIM)
        w = w_ref[g].astype(BF16)
        lng = lng_ref[:, cols]
        lnb = lnb_ref[:, cols]
        bias = bs_ref[:, g:g + 1]
        for c in range(SGU_TOK // SGU_CHUNK):
            rows = slice(c * SGU_CHUNK, (c + 1) * SGU_CHUNK)
            v = _gelu_tanh(v_ref[rows, cols])
            mu = jnp.mean(v, axis=-1, keepdims=True)
            vc = v - mu
            var = jnp.mean(vc * vc, axis=-1, keepdims=True)
            vn = vc * lax.rsqrt(var + NORM_EPS) * lng + lnb
            s = jnp.dot(w, vn.astype(BF16), preferred_element_type=F32) + bias
            o_ref[rows, cols] = (_gelu_tanh(u_ref[rows, cols]) * s).astype(o_ref.dtype)


def spatial_gating(p, ln_g, ln_b, w_s, b_s, n_rows):
    n_tiles = n_rows // SGU_TOK
    n_half = SGU_WIDTH // SGU_COLS
    bs_t = b_s.reshape(n_half, SGU_GPB, SGU_CHUNK).transpose(0, 2, 1)
    return pl.pallas_call(
        _sgu_kernel,
        grid=(n_tiles, n_half),
        in_specs=[pl.BlockSpec((SGU_TOK, SGU_COLS), lambda t, c: (t, SGU_U_COL + c)),
                  pl.BlockSpec((SGU_TOK, SGU_COLS), lambda t, c: (t, SGU_V_COL + c)),
                  pl.BlockSpec((1, SGU_COLS), lambda t, c: (0, c)),
                  pl.BlockSpec((1, SGU_COLS), lambda t, c: (0, c)),
                  pl.BlockSpec((SGU_GPB, SGU_CHUNK, SGU_CHUNK), lambda t, c: (c, 0, 0)),
                  pl.BlockSpec((None, SGU_CHUNK, SGU_GPB), lambda t, c: (c, 0, 0))],
        out_specs=pl.BlockSpec((SGU_TOK, SGU_COLS), lambda t, c: (t, c)),
        out_shape=jax.ShapeDtypeStruct((n_rows, SGU_WIDTH), BF16),
        compiler_params=_params(2),
        name="spatial_gating",
    )(p, p, ln_g.reshape(1, SGU_WIDTH), ln_b.reshape(1, SGU_WIDTH), w_s, bs_t)


ROUTER_LANES = 128


def _norm_router_kernel(x_ref, g_ref, sh_ref, sc_ref, rw_ref, f_ref, aff_ref):
    y = _rms(x_ref[...], g_ref[...])
    f = y * (1.0 + sc_ref[0]) + sh_ref[0]
    fb = f.astype(BF16)
    f_ref[...] = fb
    logits = jnp.dot(fb, rw_ref[...].astype(BF16), preferred_element_type=F32)
    lane = lax.broadcasted_iota(jnp.int32, logits.shape, 1)
    logits = jnp.where(lane < N_EXPERTS, logits, MASK_VALUE)
    e = jnp.exp(logits - jnp.max(logits, axis=1, keepdims=True))
    aff_ref[...] = e / jnp.sum(e, axis=1, keepdims=True)


def norm_router(h, g, mod, shift_k, scale_k, router_w, n_rows):
    n_tiles = n_rows // NORM_TILE
    rw = jnp.pad(router_w, ((0, 0), (0, ROUTER_LANES - N_EXPERTS)))

    def mod_map(k):
        return lambda i: (_segment(i, NORM_TILE) * N_MOD + k, 0, 0)

    return pl.pallas_call(
        _norm_router_kernel,
        grid=(n_tiles,),
        in_specs=[pl.BlockSpec((NORM_TILE, D_MODEL), lambda i: (i, 0)),
                  pl.BlockSpec((1, D_MODEL), lambda i: (0, 0)),
                  pl.BlockSpec((1, 1, D_MODEL), mod_map(shift_k)),
                  pl.BlockSpec((1, 1, D_MODEL), mod_map(scale_k)),
                  pl.BlockSpec((D_MODEL, ROUTER_LANES), lambda i: (0, 0))],
        out_specs=[pl.BlockSpec((NORM_TILE, D_MODEL), lambda i: (i, 0)),
                   pl.BlockSpec((NORM_TILE, ROUTER_LANES), lambda i: (i, 0))],
        out_shape=[jax.ShapeDtypeStruct((n_rows, D_MODEL), BF16),
                   jax.ShapeDtypeStruct((n_rows, ROUTER_LANES), F32)],
        compiler_params=_params(1),
        name="norm_router",
    )(h, g.reshape(1, D_MODEL), mod, mod, rw)


FF_CHUNK = 256
DOWN_TN = 1024


def _ffn_up_kernel(x_ref, wg_ref, wu_ref, o_ref):
    x = x_ref[...]
    gate = jnp.dot(x, wg_ref[...].astype(BF16), preferred_element_type=F32)
    up = jnp.dot(x, wu_ref[...].astype(BF16), preferred_element_type=F32)
    o_ref[...] = (_silu(gate) * up).astype(o_ref.dtype)


def _ffn_down_kernel(h_ref, wd_ref, o_ref):
    o_ref[...] = jnp.dot(h_ref[...], wd_ref[...].astype(BF16), preferred_element_type=F32)


def expert_ffn(xin, w_gate, w_up, w_down, li):
    r = xin.shape[1]
    hid = pl.pallas_call(
        _ffn_up_kernel,
        grid=(N_EXPERTS, EXPERT_FF // FF_CHUNK),
        in_specs=[pl.BlockSpec((None, r, D_MODEL), lambda e, f: (e, 0, 0)),
                  pl.BlockSpec((None, None, D_MODEL, FF_CHUNK), lambda e, f: (li, e, 0, f)),
                  pl.BlockSpec((None, None, D_MODEL, FF_CHUNK), lambda e, f: (li, e, 0, f))],
        out_specs=pl.BlockSpec((None, r, FF_CHUNK), lambda e, f: (e, 0, f)),
        out_shape=jax.ShapeDtypeStruct((N_EXPERTS, r, EXPERT_FF), BF16),
        compiler_params=_params(2),
        name="ffn_up",
    )(xin, w_gate, w_up)
    return pl.pallas_call(
        _ffn_down_kernel,
        grid=(N_EXPERTS, D_MODEL // DOWN_TN),
        in_specs=[pl.BlockSpec((None, r, EXPERT_FF), lambda e, n: (e, 0, 0)),
                  pl.BlockSpec((None, None, EXPERT_FF, DOWN_TN), lambda e, n: (li, e, 0, n))],
        out_specs=pl.BlockSpec((None, r, DOWN_TN), lambda e, n: (e, 0, n)),
        out_shape=jax.ShapeDtypeStruct((N_EXPERTS, r, D_MODEL), F32),
        compiler_params=_params(2),
        name="ffn_down",
    )(hid, w_down)


COMBINE_CHUNK = 256
COMBINE_UNROLL = 8


def _combine_kernel(rows_ref, y_ref, gate_ref, gf_ref, h_in_ref, h_ref, buf_ref, gsem, ssem, *, chunk, n_chunks):
    del h_in_ref
    e = pl.program_id(0)
    c = pl.program_id(1)
    n_e = pl.num_programs(0)
    slot = c % 2

    def copy(hbm, vmem, s, gather):
        return (pltpu.make_async_copy(hbm, vmem, gsem.at[s]) if gather
                else pltpu.make_async_copy(vmem, hbm, ssem.at[s]))

    def start_all(s, ee, cc, gather):
        base = (ee * n_chunks + cc) * chunk

        def body(g, carry):
            for u in range(COMBINE_UNROLL):
                j = g * COMBINE_UNROLL + u
                row = rows_ref[base + j]
                copy(h_ref.at[pl.ds(row, 1)], buf_ref.at[s, pl.ds(j, 1)], s, gather).start(priority=u % 2)
            return carry
        lax.fori_loop(0, chunk // COMBINE_UNROLL, body, 0)

    def wait_all(s, gather):
        copy(h_ref.at[pl.ds(0, chunk)], buf_ref.at[s], s, gather).wait()

    @pl.when((e == 0) & (c == 0))
    def _():
        start_all(0, 0, 0, True)

    wait_all(slot, True)
    buf_ref[slot] = buf_ref[slot] + gf_ref[0] * (gate_ref[...] * y_ref[...])

    @pl.when(c > 0)
    def _():
        wait_all(1 - slot, False)

    @pl.when(c < n_chunks - 1)
    def _():
        start_all(1 - slot, e, c + 1, True)
        start_all(slot, e, c, False)

    @pl.when(c == n_chunks - 1)
    def _():
        start_all(slot, e, c, False)
        wait_all(slot, False)

        @pl.when(e < n_e - 1)
        def _():
            start_all(0, e + 1, 0, True)


def moe_combine(h, y, gates, rows, mod, gate_k, seg_of_chunk, chunk, row_off):
    n_e, r = rows.shape
    n_chunks = r // chunk
    blk0 = row_off // chunk
    d = h.shape[1]
    grid_spec = pltpu.PrefetchScalarGridSpec(
        num_scalar_prefetch=1,
        grid=(n_e, n_chunks),
        in_specs=[pl.BlockSpec((None, chunk, d), lambda e, c, rows: (e, blk0 + c, 0)),
                  pl.BlockSpec((None, chunk, 1), lambda e, c, rows: (e, c, 0)),
                  pl.BlockSpec((1, 1, d), lambda e, c, rows: (seg_of_chunk(c) * N_MOD + gate_k, 0, 0)),
                  pl.BlockSpec(memory_space=pl.ANY)],
        out_specs=pl.BlockSpec(memory_space=pl.ANY),
        scratch_shapes=[pltpu.VMEM((2, chunk, d), F32),
                        pltpu.SemaphoreType.DMA((2,)), pltpu.SemaphoreType.DMA((2,))])
    return pl.pallas_call(
        functools.partial(_combine_kernel, chunk=chunk, n_chunks=n_chunks),
        grid_spec=grid_spec,
        out_shape=jax.ShapeDtypeStruct(h.shape, h.dtype),
        input_output_aliases={4: 0},
        compiler_params=_params(2),
        name="moe_combine",
    )(rows.reshape(-1), y, gates.reshape(n_e, r, 1), mod, h)


def _final_norm_kernel(x_ref, g_ref, o_ref):
    o_ref[...] = _rms(x_ref[...], g_ref[...])


def final_norm(h, g):
    return pl.pallas_call(
        _final_norm_kernel,
        grid=(N_LAT // NORM_TILE,),
        in_specs=[pl.BlockSpec((NORM_TILE, D_MODEL), lambda i: (i, 0)),
                  pl.BlockSpec((1, D_MODEL), lambda i: (0, 0))],
        out_specs=pl.BlockSpec((NORM_TILE, D_MODEL), lambda i: (i, 0)),
        out_shape=jax.ShapeDtypeStruct((N_LAT, D_MODEL), F32),
        compiler_params=_params(1),
        name="final_norm",
    )(h, g.reshape(1, D_MODEL))


def _route(aff, n_per_sample, row0):
    cap = EC_CAPACITY * n_per_sample // N_EXPERTS
    a = aff[row0:row0 + BATCH * n_per_sample, :N_EXPERTS].reshape(BATCH, n_per_sample, N_EXPERTS)
    gates, idx = lax.top_k(a.transpose(0, 2, 1), cap)
    rows = idx + (row0 + jnp.arange(BATCH) * n_per_sample)[:, None, None]
    return rows, gates


def _moe(h, f, aff, mod, big, li, update_ctx):
    def per_expert(t):
        return t.transpose(1, 0, 2).reshape(N_EXPERTS, -1)

    rows, gates = _route(aff, SEQ, 0)
    chunks_per_sample = rows.shape[2] // COMBINE_CHUNK
    parts = [(per_expert(rows), per_expert(gates), lambda c: c // chunks_per_sample, COMBINE_CHUNK)]
    if update_ctx:
        rows_c, gates_c = _route(aff, CTX_LEN, N_LAT)
        parts.append((per_expert(rows_c), per_expert(gates_c), lambda c: MOD_CTX_ROW, BATCH * rows_c.shape[2]))
    all_rows = jnp.concatenate([part[0] for part in parts], axis=1)
    y = expert_ffn(f[all_rows], big['w_gate'], big['w_up'], big['w_down'], li)
    off = 0
    for part_rows, part_gates, seg_of_chunk, chunk in parts:
        h = moe_combine(h, y, part_gates, part_rows, mod, 5, seg_of_chunk, chunk, off)
        off += part_rows.shape[1]
    return h


def _layer(h, c_rows, big, li, lp, cos_t, sin_t, update_ctx):
    n_rows = N_TOK if update_ctx else N_LAT
    mod = adaln(c_rows, big['ada_w'], lp['ada_b'], li)[:3].reshape(3 * N_MOD, 1, D_MODEL)
    a = norm_mod(h, lp['norm1_g'], mod, 0, 1)
    p = in_proj(a, big['w_in'], li)
    o_a = na_attention(p, lp['na_rpb'], with_ctx=update_ctx)
    o_f = retention_pass(p, lp['ret_log_decay'][0], cos_t, sin_t, backward=False)
    o_b = retention_pass(p, lp['ret_log_decay'][1], cos_t, sin_t, backward=True, o_fwd=o_f,
                         norm_g=lp['ret_norm_g'])
    o_c = spatial_gating(p, lp['sgu_ln_g'], lp['sgu_ln_b'], lp['sgu_w'], lp['sgu_b'], n_rows)
    h = out_proj_residual(o_a, o_b, o_c, big['w_out'], li, h, mod, 2, n_rows)
    f, aff = norm_router(h, lp['norm2_g'], mod, 3, 4, lp['router_w'], n_rows)
    return _moe(h, f, aff, mod, big, li, update_ctx)


def kernel(x, c, ctx, c_ctx, ada_w, ada_b, norm1_g, norm2_g, w_in, w_out, na_rpb, ret_log_decay, ret_norm_g,
           sgu_ln_g, sgu_ln_b, sgu_w, sgu_b, router_w, w_gate, w_up, w_down, final_g):
    h = jnp.concatenate([x.reshape(N_LAT, D_MODEL), ctx.reshape(N_CTX, D_MODEL)], axis=0)
    c_rows = jnp.zeros((ADA_ROWS, D_MODEL), F32).at[:BATCH].set(c).at[MOD_CTX_ROW].set(c_ctx)
    cos_t, sin_t = _rope_tables()
    big = {'ada_w': ada_w, 'w_in': w_in, 'w_out': w_out, 'w_gate': w_gate, 'w_up': w_up, 'w_down': w_down}
    for i in range(DEPTH):
        lp = {
            'ada_b': ada_b[i], 'norm1_g': norm1_g[i], 'norm2_g': norm2_g[i],
            'na_rpb': na_rpb[i], 'ret_log_decay': ret_log_decay[i],
            'ret_norm_g': ret_norm_g[i], 'sgu_ln_g': sgu_ln_g[i], 'sgu_ln_b': sgu_ln_b[i],
            'sgu_w': sgu_w[i], 'sgu_b': sgu_b[i], 'router_w': router_w[i],
        }
        h = _layer(h, c_rows, big, i, lp, cos_t, sin_t, update_ctx=(i < DEPTH - 1))
    return final_norm(h, final_g).reshape(BATCH, SEQ, D_MODEL)
```

```python
import functools

import numpy as np
import jax
import jax.numpy as jnp
from jax import lax
from jax.experimental import pallas as pl
from jax.experimental.pallas import tpu as pltpu

D_MODEL = 4096
BATCH = 2
SEQ = 4096
DEPTH = 2
GRID_W = 64
GRID_ROWS = SEQ // GRID_W
CTX_LEN = 256
HEAD_DIM = 128
NA_HEADS = 12
RET_HEADS = 12
SGU_GROUPS = 8
SGU_CHUNK = 128
RET_CHUNK = 128
WIN_ROWS = 8
WIN_COLS = 16
NA_WIDTH = NA_HEADS * HEAD_DIM
RET_WIDTH = RET_HEADS * HEAD_DIM
SGU_WIDTH = SGU_GROUPS * HEAD_DIM
MIX_WIDTH = NA_WIDTH + RET_WIDTH + SGU_WIDTH
IN_WIDTH = 3 * NA_WIDTH + 4 * RET_WIDTH + 2 * SGU_WIDTH
N_EXPERTS = 16
EXPERT_FF = 1024
EC_CAPACITY = 2
N_MOD = 6
ROPE_BASE = 10000.0
NORM_EPS = 1e-6

N_CTX = BATCH * CTX_LEN
N_LAT = BATCH * SEQ
N_TOK = N_CTX + N_LAT
MOD_CTX_ROW = BATCH
ATT_SCALE = HEAD_DIM ** -0.5
MASK_VALUE = -1e30

Q_ROWS = 4
Q_BLK = Q_ROWS * GRID_W
K_BAND_ROWS = 12
K_BAND = K_BAND_ROWS * GRID_W
N_QBLK = GRID_ROWS // Q_ROWS

VMEM_LIMIT = 56 * 1024 * 1024

F32 = jnp.float32
BF16 = jnp.bfloat16


def _params(n_axes, vmem=VMEM_LIMIT):
    return pltpu.CompilerParams(dimension_semantics=("arbitrary",) * n_axes, vmem_limit_bytes=vmem)


def _segment(tile_idx, tile_rows):
    return jnp.where(tile_idx < N_LAT // tile_rows, tile_idx // (SEQ // tile_rows), MOD_CTX_ROW)


def _silu(x):
    return x / (1.0 + jnp.exp(-x))


def _gelu_tanh(x):
    return x * (0.5 * (1.0 + jnp.tanh(np.float32(np.sqrt(2.0 / np.pi)) * (x + 0.044715 * (x * x * x)))))


ADA_TN = 1024
ADA_ROWS = 16


def _adaln_kernel(c_ref, w_ref, b_ref, o_ref):
    s = _silu(c_ref[...]).astype(BF16)
    o_ref[...] = jnp.dot(s, w_ref[...].astype(BF16), preferred_element_type=F32) + b_ref[...]


def adaln(c_rows, ada_w, ada_b, li):
    n = ada_w.shape[2]
    return pl.pallas_call(
        _adaln_kernel,
        grid=(n // ADA_TN,),
        in_specs=[pl.BlockSpec((ADA_ROWS, D_MODEL), lambda j: (0, 0)),
                  pl.BlockSpec((None, D_MODEL, ADA_TN), lambda j: (li, 0, j)),
                  pl.BlockSpec((1, ADA_TN), lambda j: (0, j))],
        out_specs=pl.BlockSpec((ADA_ROWS, ADA_TN), lambda j: (0, j)),
        out_shape=jax.ShapeDtypeStruct((ADA_ROWS, n), F32),
        compiler_params=_params(1),
        name="adaln",
    )(c_rows, ada_w, ada_b.reshape(1, n))


NORM_TILE = 256


def _rms(x, g):
    return x * lax.rsqrt(jnp.mean(x * x, axis=-1, keepdims=True) + NORM_EPS) * g


def _norm_mod_kernel(x_ref, g_ref, sh_ref, sc_ref, o_ref):
    y = _rms(x_ref[...], g_ref[...])
    o_ref[...] = (y * (1.0 + sc_ref[0]) + sh_ref[0]).astype(o_ref.dtype)


def norm_mod(h, g, mod, shift_k, scale_k):
    def mod_map(k):
        return lambda i: (_segment(i, NORM_TILE) * N_MOD + k, 0, 0)

    return pl.pallas_call(
        _norm_mod_kernel,
        grid=(N_TOK // NORM_TILE,),
        in_specs=[pl.BlockSpec((NORM_TILE, D_MODEL), lambda i: (i, 0)),
                  pl.BlockSpec((1, D_MODEL), lambda i: (0, 0)),
                  pl.BlockSpec((1, 1, D_MODEL), mod_map(shift_k)),
                  pl.BlockSpec((1, 1, D_MODEL), mod_map(scale_k))],
        out_specs=pl.BlockSpec((NORM_TILE, D_MODEL), lambda i: (i, 0)),
        out_shape=jax.ShapeDtypeStruct((N_TOK, D_MODEL), BF16),
        compiler_params=_params(1),
        name="norm_mod",
    )(h, g.reshape(1, D_MODEL), mod, mod)


def _norm_mod_first_kernel(x_ref, c_ref, g_ref, sh_ref, sc_ref, o_ref, h_ref):
    src = jnp.where(pl.program_id(0) < N_LAT // NORM_TILE, x_ref[...], c_ref[...])
    h_ref[...] = src
    y = _rms(src, g_ref[...])
    o_ref[...] = (y * (1.0 + sc_ref[0]) + sh_ref[0]).astype(o_ref.dtype)


def norm_mod_first(x2d, ctx2d, g, mod, shift_k, scale_k):
    n_lat_tiles = N_LAT // NORM_TILE

    def mod_map(k):
        return lambda i: (_segment(i, NORM_TILE) * N_MOD + k, 0, 0)

    return pl.pallas_call(
        _norm_mod_first_kernel,
        grid=(N_TOK // NORM_TILE,),
        in_specs=[pl.BlockSpec((NORM_TILE, D_MODEL), lambda i: (jnp.minimum(i, n_lat_tiles - 1), 0)),
                  pl.BlockSpec((NORM_TILE, D_MODEL), lambda i: (jnp.maximum(i - n_lat_tiles, 0), 0)),
                  pl.BlockSpec((1, D_MODEL), lambda i: (0, 0)),
                  pl.BlockSpec((1, 1, D_MODEL), mod_map(shift_k)),
                  pl.BlockSpec((1, 1, D_MODEL), mod_map(scale_k))],
        out_specs=[pl.BlockSpec((NORM_TILE, D_MODEL), lambda i: (i, 0)),
                   pl.BlockSpec((NORM_TILE, D_MODEL), lambda i: (i, 0))],
        out_shape=[jax.ShapeDtypeStruct((N_TOK, D_MODEL), BF16),
                   jax.ShapeDtypeStruct((N_TOK, D_MODEL), F32)],
        compiler_params=_params(1),
        name="norm_mod_first",
    )(x2d, ctx2d, g.reshape(1, D_MODEL), mod, mod)


PROJ_TM = 2176
PROJ_TN = 256


_PROJ_F32_COLS = ((0, NA_WIDTH), (3 * NA_WIDTH, 3 * NA_WIDTH + 2 * RET_WIDTH),
                  (3 * NA_WIDTH + 3 * RET_WIDTH, IN_WIDTH))
_PROJ_BF16_COLS = ((NA_WIDTH, 3 * NA_WIDTH), (3 * NA_WIDTH + 2 * RET_WIDTH, 3 * NA_WIDTH + 3 * RET_WIDTH))


def _proj_kernel(tile_ref, a_ref, w_ref, o_ref):
    del tile_ref
    o_ref[...] = jnp.dot(a_ref[...], w_ref[...].astype(BF16), preferred_element_type=F32).astype(o_ref.dtype)


def _proj_call(a, w_in, li, col_ranges, dtype, name):
    tile_of = np.concatenate([np.arange(lo // PROJ_TN, hi // PROJ_TN) for lo, hi in col_ranges]).astype(np.int32)
    width = PROJ_TN * len(tile_of)
    grid_spec = pltpu.PrefetchScalarGridSpec(
        num_scalar_prefetch=1,
        grid=(N_TOK // PROJ_TM, len(tile_of)),
        in_specs=[pl.BlockSpec((PROJ_TM, D_MODEL), lambda m, n, t: (m, 0), pipeline_mode=pl.Buffered(1)),
                  pl.BlockSpec((None, D_MODEL, PROJ_TN), lambda m, n, t: (li, 0, t[n]))],
        out_specs=pl.BlockSpec((PROJ_TM, PROJ_TN), lambda m, n, t: (m, n)))
    return pl.pallas_call(
        _proj_kernel,
        grid_spec=grid_spec,
        out_shape=jax.ShapeDtypeStruct((N_TOK, width), dtype),
        compiler_params=_params(2),
        name=name,
    )(jnp.asarray(tile_of), a, w_in)


def in_proj(a, w_in, li):
    return (_proj_call(a, w_in, li, _PROJ_F32_COLS, F32, "in_proj_f32"),
            _proj_call(a, w_in, li, _PROJ_BF16_COLS, BF16, "in_proj_bf16"))


OUT_TN = 256


def _out_proj_kernel(oa_ref, ob_ref, oc_ref, w_ref, h_ref, g0_ref, g1_ref, gc_ref, o_ref):
    w = w_ref[...].astype(BF16)
    y = jnp.dot(oa_ref[...], w[0:NA_WIDTH], preferred_element_type=F32)
    y = y + jnp.dot(ob_ref[...], w[NA_WIDTH:NA_WIDTH + RET_WIDTH], preferred_element_type=F32)
    y = y + jnp.dot(oc_ref[...], w[NA_WIDTH + RET_WIDTH:MIX_WIDTH], preferred_element_type=F32)
    tm = o_ref.shape[0]
    row = pl.program_id(0) * tm + lax.broadcasted_iota(jnp.int32, (tm, 1), 0)
    gate = jnp.where(row < SEQ, g0_ref[0], jnp.where(row < N_LAT, g1_ref[0], gc_ref[0]))
    o_ref[...] = h_ref[...] + gate * y


def out_proj_residual(o_a, o_b, o_c, w_out, li, h, mod, gate_k, n_rows):
    tm = n_rows // 4

    def gate_map(seg):
        return lambda m, n: (seg * N_MOD + gate_k, 0, n)

    def row_tile(width):
        return pl.BlockSpec((tm, width), lambda m, n: (m, 0), pipeline_mode=pl.Buffered(1))

    gate_spec = [pl.BlockSpec((1, 1, OUT_TN), gate_map(seg)) for seg in (0, 1, MOD_CTX_ROW)]
    return pl.pallas_call(
        _out_proj_kernel,
        grid=(n_rows // tm, D_MODEL // OUT_TN),
        in_specs=[row_tile(NA_WIDTH), row_tile(RET_WIDTH), row_tile(SGU_WIDTH),
                  pl.BlockSpec((None, MIX_WIDTH, OUT_TN), lambda m, n: (li, 0, n)),
                  pl.BlockSpec((tm, OUT_TN), lambda m, n: (m, n))] + gate_spec,
        out_specs=pl.BlockSpec((tm, OUT_TN), lambda m, n: (m, n)),
        out_shape=jax.ShapeDtypeStruct((N_TOK, D_MODEL), F32),
        input_output_aliases={4: 0},
        compiler_params=_params(2),
        name="out_proj",
    )(o_a, o_b, o_c, w_out, h, mod, mod, mod)


def _na_kernel(q_ref, k0_ref, k1_ref, k2_ref, kc_ref, v0_ref, v1_ref, v2_ref, vc_ref, bias_ref, o_ref):
    for h in range(NA_HEADS):
        cols = slice(h * HEAD_DIM, (h + 1) * HEAD_DIM)
        q = (q_ref[:, cols] * ATT_SCALE).astype(BF16)

        def scores(k_ref):
            k = k_ref[:, cols].astype(BF16)
            return lax.dot_general(q, k, (((1,), (1,)), ((), ())), preferred_element_type=F32)

        s_win = jnp.concatenate([scores(k0_ref), scores(k1_ref), scores(k2_ref)], axis=1) + bias_ref[h]
        s_ctx = scores(kc_ref)
        m = jnp.maximum(jnp.max(s_win, axis=1, keepdims=True), jnp.max(s_ctx, axis=1, keepdims=True))
        e_win = jnp.exp(s_win - m)
        e_ctx = jnp.exp(s_ctx - m)
        denom = jnp.sum(e_win, axis=1, keepdims=True) + jnp.sum(e_ctx, axis=1, keepdims=True)
        acc = jnp.dot(e_ctx.astype(BF16), vc_ref[:, cols].astype(BF16), preferred_element_type=F32)
        for t, v_ref in enumerate((v0_ref, v1_ref, v2_ref)):
            e = e_win[:, t * Q_BLK:(t + 1) * Q_BLK].astype(BF16)
            acc = acc + jnp.dot(e, v_ref[:, cols].astype(BF16), preferred_element_type=F32)
        o_ref[:, cols] = (acc / denom).astype(o_ref.dtype)


def _na_bias_table(rpb):
    n_off = 2 * WIN_ROWS - 1
    pad = GRID_W - WIN_COLS
    ext = jnp.pad(rpb.astype(F32), ((0, 0), (0, 0), (pad, pad)))
    toep = jnp.stack([ext[..., GRID_W - 1 - qc:2 * GRID_W - 1 - qc] for qc in range(GRID_W)], axis=-2)
    qc = np.arange(GRID_W)
    c0 = np.clip(qc - WIN_COLS // 2, 0, GRID_W - WIN_COLS)
    ok_c = (qc[None, :] >= c0[:, None]) & (qc[None, :] < c0[:, None] + WIN_COLS)
    toep = jnp.where(ok_c, toep, MASK_VALUE)
    toep = jnp.concatenate([toep, jnp.full((NA_HEADS, 1, GRID_W, GRID_W), MASK_VALUE, F32)], axis=1)
    off = np.full((4, Q_ROWS, K_BAND_ROWS), n_off, np.int32)
    for pat, blk in enumerate((0, 1, N_QBLK - 1)):
        band0 = int(np.clip(Q_ROWS * blk - Q_ROWS, 0, GRID_ROWS - K_BAND_ROWS))
        for q in range(Q_ROWS):
            qr = Q_ROWS * blk + q
            r0 = int(np.clip(qr - WIN_ROWS // 2, 0, GRID_ROWS - WIN_ROWS))
            for k in range(K_BAND_ROWS):
                kr = band0 + k
                if r0 <= kr < r0 + WIN_ROWS:
                    off[pat, q, k] = kr - qr + (WIN_ROWS - 1)
    tiles = jnp.take(toep, off.reshape(-1), axis=1)
    tiles = tiles.reshape(NA_HEADS, 4, Q_ROWS, K_BAND_ROWS, GRID_W, GRID_W)
    return tiles.transpose(1, 0, 2, 4, 3, 5).reshape(4, NA_HEADS, Q_BLK, K_BAND)


def na_attention(p32, p16, rpb, with_ctx):
    bias = _na_bias_table(rpb)
    lead = 1 if with_ctx else 0

    def lat_block(b, j):
        return b * N_QBLK + j

    def ctx_block(b):
        return N_LAT // Q_BLK + b

    def q_map(b, i):
        return (jnp.where(i < lead, ctx_block(b), lat_block(b, i - lead)), 0)

    def band_map(t, col):
        def f(b, i):
            return (lat_block(b, jnp.clip(i - lead - 1, 0, N_QBLK - 3)) + t, col)
        return f

    def ctx_map(col):
        return lambda b, i: (ctx_block(b), col)

    def bias_map(b, i):
        j = i - lead
        return (jnp.where(j < 0, 3, jnp.where(j == 0, 0, jnp.where(j == N_QBLK - 1, 2, 1))), 0, 0, 0)

    def out_map(b, i):
        return (jnp.where(i < lead, ctx_block(b), lat_block(b, i - lead)), 0)

    blk = (Q_BLK, NA_WIDTH)
    return pl.pallas_call(
        _na_kernel,
        grid=(BATCH, N_QBLK + lead),
        in_specs=[pl.BlockSpec(blk, q_map),
                  pl.BlockSpec(blk, band_map(0, 0)), pl.BlockSpec(blk, band_map(1, 0)),
                  pl.BlockSpec(blk, band_map(2, 0)), pl.BlockSpec(blk, ctx_map(0)),
                  pl.BlockSpec(blk, band_map(0, 1)), pl.BlockSpec(blk, band_map(1, 1)),
                  pl.BlockSpec(blk, band_map(2, 1)), pl.BlockSpec(blk, ctx_map(1)),
                  pl.BlockSpec((None, NA_HEADS, Q_BLK, K_BAND), bias_map, pipeline_mode=pl.Buffered(1))],
        out_specs=pl.BlockSpec(blk, out_map),
        out_shape=jax.ShapeDtypeStruct((N_TOK if with_ctx else N_LAT, NA_WIDTH), BF16),
        compiler_params=_params(2),
        name="na_attention",
    )(p32, p16, p16, p16, p16, p16, p16, p16, p16, bias)


CTX_CHUNKS = CTX_LEN // RET_CHUNK
LAT_CHUNKS = SEQ // RET_CHUNK
RET_STEPS = CTX_CHUNKS + LAT_CHUNKS
RET_Q_COL, RET_K_COL, RET_G_COL = 1, 2, 3
RET_V_COL = 2


def _ret_kernel(ld_ref, q_ref, k_ref, v_ref, cos_ref, sin_ref, *rest, backward):
    if backward:
        of_ref, g_ref, ng_ref, o_ref, st_ref = rest
    else:
        o_ref, st_ref = rest

    @pl.when(pl.program_id(1) == 0)
    def _():
        st_ref[...] = jnp.zeros_like(st_ref)

    ii = lax.broadcasted_iota(jnp.int32, (RET_CHUNK, RET_CHUNK), 0)
    jj = lax.broadcasted_iota(jnp.int32, (RET_CHUNK, RET_CHUNK), 1)
    diff = ((jj - ii) if backward else (ii - jj)).astype(F32)
    causal = diff >= 0.0
    diff = jnp.maximum(diff, 0.0)
    row = lax.broadcasted_iota(jnp.int32, (RET_CHUNK, 1), 0)
    pos = ((RET_CHUNK - 1 - row) if backward else row).astype(F32)
    first_half = (jj % (HEAD_DIM // 2)) < (HEAD_DIM // 4)
    cos = cos_ref[...]
    sin = sin_ref[...]

    def rope(x):
        partner = jnp.where(first_half, pltpu.roll(x, HEAD_DIM - HEAD_DIM // 4, axis=1),
                            pltpu.roll(x, HEAD_DIM // 4, axis=1))
        return x * cos + partner * sin

    for hh in range(RET_HEADS):
        cols = slice(hh * HEAD_DIM, (hh + 1) * HEAD_DIM)
        ld = ld_ref[hh]
        inner_decay = jnp.where(causal, jnp.exp(diff * ld), 0.0)
        q_decay = jnp.exp((pos + 1.0) * ld)
        k_decay = jnp.exp((RET_CHUNK - 1.0 - pos) * ld)
        chunk_decay = jnp.exp(jnp.full((1, HEAD_DIM), RET_CHUNK, F32) * ld)
        q = rope(q_ref[:, cols])
        k = rope(k_ref[:, cols]) * ATT_SCALE
        qb = q.astype(BF16)
        vb = v_ref[:, cols].astype(BF16)
        att = lax.dot_general(qb, k.astype(BF16), (((1,), (1,)), ((), ())), preferred_element_type=F32)
        inner = jnp.dot((att * inner_decay).astype(BF16), vb, preferred_element_type=F32)
        state = st_ref[hh]
        cross = jnp.dot(qb, state.astype(BF16), preferred_element_type=F32) * q_decay
        kd_t = (k * k_decay).T.astype(BF16)
        st_ref[hh] = state * chunk_decay + jnp.dot(kd_t, vb, preferred_element_type=F32)
        o = inner + cross
        if backward:
            tot = of_ref[:, cols] + o
            y = _rms(tot, ng_ref[:, cols]) * _silu(g_ref[:, cols])
            o_ref[:, cols] = y.astype(o_ref.dtype)
        else:
            o_ref[:, cols] = o


def _ret_chunk_block(b, s, backward):
    ctx0 = N_LAT // RET_CHUNK + b * CTX_CHUNKS
    lat0 = b * LAT_CHUNKS
    if backward:
        ctx_blk = ctx0 + (CTX_CHUNKS - 1 - s)
        lat_blk = lat0 + (LAT_CHUNKS - 1 - (s - CTX_CHUNKS))
    else:
        ctx_blk = ctx0 + s
        lat_blk = lat0 + (s - CTX_CHUNKS)
    return jnp.where(s < CTX_CHUNKS, ctx_blk, lat_blk)


def retention_pass(p32, p16, log_decay, cos_t, sin_t, backward, o_fwd=None, norm_g=None):
    def col_map(col):
        return lambda b, s: (_ret_chunk_block(b, s, backward), col)

    wide = (RET_CHUNK, RET_WIDTH)
    in_specs = [pl.BlockSpec(memory_space=pltpu.SMEM),
                pl.BlockSpec(wide, col_map(RET_Q_COL)), pl.BlockSpec(wide, col_map(RET_K_COL)),
                pl.BlockSpec(wide, col_map(RET_V_COL)),
                pl.BlockSpec((RET_CHUNK, HEAD_DIM), col_map(0)), pl.BlockSpec((RET_CHUNK, HEAD_DIM), col_map(0))]
    args = [log_decay, p32, p32, p16, cos_t, sin_t]
    if backward:
        in_specs += [pl.BlockSpec(wide, col_map(0)), pl.BlockSpec(wide, col_map(RET_G_COL)),
                     pl.BlockSpec((1, RET_WIDTH), lambda b, s: (0, 0))]
        args += [o_fwd, p32, norm_g.reshape(1, RET_WIDTH)]
    return pl.pallas_call(
        functools.partial(_ret_kernel, backward=backward),
        grid=(BATCH, RET_STEPS),
        in_specs=in_specs,
        out_specs=pl.BlockSpec(wide, col_map(0)),
        out_shape=jax.ShapeDtypeStruct((N_TOK, RET_WIDTH), BF16 if backward else F32),
        scratch_shapes=[pltpu.VMEM((RET_HEADS, HEAD_DIM, HEAD_DIM), F32)],
        compiler_params=_params(2),
        name="retention_bwd" if backward else "retention_fwd",
    )(*args)


def _rope_tables():
    quarter = HEAD_DIM // 4
    inv_freq = np.float64(ROPE_BASE) ** (-np.arange(quarter, dtype=np.float64) / quarter)
    t = np.arange(SEQ)
    ang_r = (t // GRID_W)[:, None] * inv_freq[None, :]
    ang_c = (t % GRID_W)[:, None] * inv_freq[None, :]
    cos = np.concatenate([np.cos(ang_r), np.cos(ang_r), np.cos(ang_c), np.cos(ang_c)], axis=-1)
    sin = np.concatenate([-np.sin(ang_r), np.sin(ang_r), -np.sin(ang_c), np.sin(ang_c)], axis=-1)
    cos = np.concatenate([cos] * BATCH + [np.ones((N_CTX, HEAD_DIM))], axis=0).astype(np.float32)
    sin = np.concatenate([sin] * BATCH + [np.zeros((N_CTX, HEAD_DIM))], axis=0).astype(np.float32)
    return jnp.asarray(cos), jnp.asarray(sin)


SGU_TOK = 512
SGU_COLS = 512
SGU_U_COL = (NA_WIDTH + 3 * RET_WIDTH) // SGU_COLS
SGU_V_COL = SGU_U_COL + SGU_WIDTH // SGU_COLS
SGU_GPB = SGU_COLS // HEAD_DIM


def _sgu_kernel(u_ref, v_ref, lng_ref, lnb_ref, w_ref, bs_ref, o_ref):
    for g in range(SGU_GPB):
        cols = slice(g * HEAD_DIM, (g + 1) * HEAD_DIM)
        w = w_ref[g].astype(BF16)
        lng = lng_ref[:, cols]
        lnb = lnb_ref[:, cols]
        bias = bs_ref[:, g:g + 1]
        for c in range(SGU_TOK // SGU_CHUNK):
            rows = slice(c * SGU_CHUNK, (c + 1) * SGU_CHUNK)
            v = _gelu_tanh(v_ref[rows, cols])
            mu = jnp.mean(v, axis=-1, keepdims=True)
            vc = v - mu
            var = jnp.mean(vc * vc, axis=-1, keepdims=True)
            vn = vc * lax.rsqrt(var + NORM_EPS) * lng + lnb
            s = jnp.dot(w, vn.astype(BF16), preferred_element_type=F32) + bias
            o_ref[rows, cols] = (_gelu_tanh(u_ref[rows, cols]) * s).astype(o_ref.dtype)


def spatial_gating(p, ln_g, ln_b, w_s, b_s, n_rows):
    n_tiles = n_rows // SGU_TOK
    n_half = SGU_WIDTH // SGU_COLS
    bs_t = b_s.reshape(n_half, SGU_GPB, SGU_CHUNK).transpose(0, 2, 1)
    return pl.pallas_call(
        _sgu_kernel,
        grid=(n_tiles, n_half),
        in_specs=[pl.BlockSpec((SGU_TOK, SGU_COLS), lambda t, c: (t, SGU_U_COL + c)),
                  pl.BlockSpec((SGU_TOK, SGU_COLS), lambda t, c: (t, SGU_V_COL + c)),
                  pl.BlockSpec((1, SGU_COLS), lambda t, c: (0, c)),
                  pl.BlockSpec((1, SGU_COLS), lambda t, c: (0, c)),
                  pl.BlockSpec((SGU_GPB, SGU_CHUNK, SGU_CHUNK), lambda t, c: (c, 0, 0)),
                  pl.BlockSpec((None, SGU_CHUNK, SGU_GPB), lambda t, c: (c, 0, 0))],
        out_specs=pl.BlockSpec((SGU_TOK, SGU_COLS), lambda t, c: (t, c)),
        out_shape=jax.ShapeDtypeStruct((n_rows, SGU_WIDTH), BF16),
        compiler_params=_params(2),
        name="spatial_gating",
    )(p, p, ln_g.reshape(1, SGU_WIDTH), ln_b.reshape(1, SGU_WIDTH), w_s, bs_t)


ROUTER_LANES = 128


def _norm_router_kernel(x_ref, g_ref, sh_ref, sc_ref, rw_ref, f_ref, aff_ref):
    y = _rms(x_ref[...], g_ref[...])
    f = y * (1.0 + sc_ref[0]) + sh_ref[0]
    fb = f.astype(BF16)
    f_ref[...] = fb
    logits = jnp.dot(fb, rw_ref[...].astype(BF16), preferred_element_type=F32)
    lane = lax.broadcasted_iota(jnp.int32, logits.shape, 1)
    logits = jnp.where(lane < N_EXPERTS, logits, MASK_VALUE)
    e = jnp.exp(logits - jnp.max(logits, axis=1, keepdims=True))
    aff_ref[...] = e / jnp.sum(e, axis=1, keepdims=True)


def norm_router(h, g, mod, shift_k, scale_k, router_w, n_rows):
    n_tiles = n_rows // NORM_TILE
    rw = jnp.pad(router_w, ((0, 0), (0, ROUTER_LANES - N_EXPERTS)))

    def mod_map(k):
        return lambda i: (_segment(i, NORM_TILE) * N_MOD + k, 0, 0)

    return pl.pallas_call(
        _norm_router_kernel,
        grid=(n_tiles,),
        in_specs=[pl.BlockSpec((NORM_TILE, D_MODEL), lambda i: (i, 0)),
                  pl.BlockSpec((1, D_MODEL), lambda i: (0, 0)),
                  pl.BlockSpec((1, 1, D_MODEL), mod_map(shift_k)),
                  pl.BlockSpec((1, 1, D_MODEL), mod_map(scale_k)),
                  pl.BlockSpec((D_MODEL, ROUTER_LANES), lambda i: (0, 0))],
        out_specs=[pl.BlockSpec((NORM_TILE, D_MODEL), lambda i: (i, 0)),
                   pl.BlockSpec((NORM_TILE, ROUTER_LANES), lambda i: (i, 0))],
        out_shape=[jax.ShapeDtypeStruct((n_rows, D_MODEL), BF16),
                   jax.ShapeDtypeStruct((n_rows, ROUTER_LANES), F32)],
        compiler_params=_params(1),
        name="norm_router",
    )(h, g.reshape(1, D_MODEL), mod, mod, rw)


FF_CHUNK = 256
DOWN_TN = 1024


def _ffn_up_kernel(x_ref, wg_ref, wu_ref, o_ref):
    x = x_ref[...]
    gate = jnp.dot(x, wg_ref[...].astype(BF16), preferred_element_type=F32)
    up = jnp.dot(x, wu_ref[...].astype(BF16), preferred_element_type=F32)
    o_ref[...] = (_silu(gate) * up).astype(o_ref.dtype)


def _ffn_down_kernel(h_ref, wd_ref, o_ref):
    o_ref[...] = jnp.dot(h_ref[...], wd_ref[...].astype(BF16), preferred_element_type=F32)


def expert_ffn(xin, w_gate, w_up, w_down, li):
    r = xin.shape[1]
    hid = pl.pallas_call(
        _ffn_up_kernel,
        grid=(N_EXPERTS, EXPERT_FF // FF_CHUNK),
        in_specs=[pl.BlockSpec((None, r, D_MODEL), lambda e, f: (e, 0, 0)),
                  pl.BlockSpec((None, None, D_MODEL, FF_CHUNK), lambda e, f: (li, e, 0, f)),
                  pl.BlockSpec((None, None, D_MODEL, FF_CHUNK), lambda e, f: (li, e, 0, f))],
        out_specs=pl.BlockSpec((None, r, FF_CHUNK), lambda e, f: (e, 0, f)),
        out_shape=jax.ShapeDtypeStruct((N_EXPERTS, r, EXPERT_FF), BF16),
        compiler_params=_params(2),
        name="ffn_up",
    )(xin, w_gate, w_up)
    return pl.pallas_call(
        _ffn_down_kernel,
        grid=(N_EXPERTS, D_MODEL // DOWN_TN),
        in_specs=[pl.BlockSpec((None, r, EXPERT_FF), lambda e, n: (e, 0, 0)),
                  pl.BlockSpec((None, None, EXPERT_FF, DOWN_TN), lambda e, n: (li, e, 0, n))],
        out_specs=pl.BlockSpec((None, r, DOWN_TN), lambda e, n: (e, 0, n)),
        out_shape=jax.ShapeDtypeStruct((N_EXPERTS, r, D_MODEL), F32),
        compiler_params=_params(2),
        name="ffn_down",
    )(hid, w_down)


COMBINE_CHUNK = 256
COMBINE_UNROLL = 8


def _combine_kernel(rows_ref, y_ref, gate_ref, gf_ref, h_in_ref, h_ref, buf_ref, gsem, ssem, *, chunk, n_chunks):
    del h_in_ref
    e = pl.program_id(0)
    c = pl.program_id(1)
    n_e = pl.num_programs(0)
    slot = c % 2

    def copy(hbm, vmem, s, gather):
        return (pltpu.make_async_copy(hbm, vmem, gsem.at[s]) if gather
                else pltpu.make_async_copy(vmem, hbm, ssem.at[s]))

    def start_all(s, ee, cc, gather):
        base = (ee * n_chunks + cc) * chunk

        def body(g, carry):
            for u in range(COMBINE_UNROLL):
                j = g * COMBINE_UNROLL + u
                row = rows_ref[base + j]
                copy(h_ref.at[pl.ds(row, 1)], buf_ref.at[s, pl.ds(j, 1)], s, gather).start(priority=u % 2)
            return carry
        lax.fori_loop(0, chunk // COMBINE_UNROLL, body, 0)

    def wait_all(s, gather):
        copy(h_ref.at[pl.ds(0, chunk)], buf_ref.at[s], s, gather).wait()

    @pl.when((e == 0) & (c == 0))
    def _():
        start_all(0, 0, 0, True)

    wait_all(slot, True)
    buf_ref[slot] = buf_ref[slot] + gf_ref[0] * (gate_ref[...] * y_ref[...])

    @pl.when(c > 0)
    def _():
        wait_all(1 - slot, False)

    @pl.when(c < n_chunks - 1)
    def _():
        start_all(1 - slot, e, c + 1, True)
        start_all(slot, e, c, False)

    @pl.when(c == n_chunks - 1)
    def _():
        start_all(slot, e, c, False)
        wait_all(slot, False)

        @pl.when(e < n_e - 1)
        def _():
            start_all(0, e + 1, 0, True)


def moe_combine(h, y, gates, rows, mod, gate_k, seg_of_chunk, chunk, row_off):
    n_e, r = rows.shape
    n_chunks = r // chunk
    blk0 = row_off // chunk
    d = h.shape[1]
    grid_spec = pltpu.PrefetchScalarGridSpec(
        num_scalar_prefetch=1,
        grid=(n_e, n_chunks),
        in_specs=[pl.BlockSpec((None, chunk, d), lambda e, c, rows: (e, blk0 + c, 0)),
                  pl.BlockSpec((None, chunk, 1), lambda e, c, rows: (e, c, 0)),
                  pl.BlockSpec((1, 1, d), lambda e, c, rows: (seg_of_chunk(c) * N_MOD + gate_k, 0, 0)),
                  pl.BlockSpec(memory_space=pl.ANY)],
        out_specs=pl.BlockSpec(memory_space=pl.ANY),
        scratch_shapes=[pltpu.VMEM((2, chunk, d), F32),
                        pltpu.SemaphoreType.DMA((2,)), pltpu.SemaphoreType.DMA((2,))])
    return pl.pallas_call(
        functools.partial(_combine_kernel, chunk=chunk, n_chunks=n_chunks),
        grid_spec=grid_spec,
        out_shape=jax.ShapeDtypeStruct(h.shape, h.dtype),
        input_output_aliases={4: 0},
        compiler_params=_params(2),
        name="moe_combine",
    )(rows.reshape(-1), y, gates.reshape(n_e, r, 1), mod, h)


def _final_norm_kernel(x_ref, g_ref, o_ref):
    o_ref[...] = _rms(x_ref[...], g_ref[...])


def final_norm(h, g):
    return pl.pallas_call(
        _final_norm_kernel,
        grid=(N_LAT // NORM_TILE,),
        in_specs=[pl.BlockSpec((NORM_TILE, D_MODEL), lambda i: (i, 0)),
                  pl.BlockSpec((1, D_MODEL), lambda i: (0, 0))],
        out_specs=pl.BlockSpec((NORM_TILE, D_MODEL), lambda i: (i, 0)),
        out_shape=jax.ShapeDtypeStruct((N_LAT, D_MODEL), F32),
        compiler_params=_params(1),
        name="final_norm",
    )(h, g.reshape(1, D_MODEL))


def _route(aff, n_per_sample, row0):
    cap = EC_CAPACITY * n_per_sample // N_EXPERTS
    a = aff[row0:row0 + BATCH * n_per_sample, :N_EXPERTS].reshape(BATCH, n_per_sample, N_EXPERTS)
    gates, idx = lax.top_k(a.transpose(0, 2, 1), cap)
    rows = idx + (row0 + jnp.arange(BATCH) * n_per_sample)[:, None, None]
    return rows, gates


def _moe(h, f, aff, mod, big, li, update_ctx):
    def per_expert(t):
        return t.transpose(1, 0, 2).reshape(N_EXPERTS, -1)

    rows, gates = _route(aff, SEQ, 0)
    chunks_per_sample = rows.shape[2] // COMBINE_CHUNK
    parts = [(per_expert(rows), per_expert(gates), lambda c: c // chunks_per_sample, COMBINE_CHUNK)]
    if update_ctx:
        rows_c, gates_c = _route(aff, CTX_LEN, N_LAT)
        parts.append((per_expert(rows_c), per_expert(gates_c), lambda c: MOD_CTX_ROW, BATCH * rows_c.shape[2]))
    all_rows = jnp.concatenate([part[0] for part in parts], axis=1)
    y = expert_ffn(f[all_rows], big['w_gate'], big['w_up'], big['w_down'], li)
    off = 0
    for part_rows, part_gates, seg_of_chunk, chunk in parts:
        h = moe_combine(h, y, part_gates, part_rows, mod, 5, seg_of_chunk, chunk, off)
        off += part_rows.shape[1]
    return h


def _layer(h, c_rows, big, li, lp, cos_t, sin_t, update_ctx):
    n_rows = N_TOK if update_ctx else N_LAT
    mod = adaln(c_rows, big['ada_w'], lp['ada_b'], li)[:3].reshape(3 * N_MOD, 1, D_MODEL)
    if isinstance(h, tuple):
        a, h = norm_mod_first(h[0], h[1], lp['norm1_g'], mod, 0, 1)
    else:
        a = norm_mod(h, lp['norm1_g'], mod, 0, 1)
    p32, p16 = in_proj(a, big['w_in'], li)
    o_a = na_attention(p32, p16, lp['na_rpb'], with_ctx=update_ctx)
    o_f = retention_pass(p32, p16, lp['ret_log_decay'][0], cos_t, sin_t, backward=False)
    o_b = retention_pass(p32, p16, lp['ret_log_decay'][1], cos_t, sin_t, backward=True, o_fwd=o_f,
                         norm_g=lp['ret_norm_g'])
    o_c = spatial_gating(p32, lp['sgu_ln_g'], lp['sgu_ln_b'], lp['sgu_w'], lp['sgu_b'], n_rows)
    h = out_proj_residual(o_a, o_b, o_c, big['w_out'], li, h, mod, 2, n_rows)
    f, aff = norm_router(h, lp['norm2_g'], mod, 3, 4, lp['router_w'], n_rows)
    return _moe(h, f, aff, mod, big, li, update_ctx)


def kernel(x, c, ctx, c_ctx, ada_w, ada_b, norm1_g, norm2_g, w_in, w_out, na_rpb, ret_log_decay, ret_norm_g,
           sgu_ln_g, sgu_ln_b, sgu_w, sgu_b, router_w, w_gate, w_up, w_down, final_g):
    h = (x.reshape(N_LAT, D_MODEL), ctx.reshape(N_CTX, D_MODEL))
    c_rows = jnp.zeros((ADA_ROWS, D_MODEL), F32).at[:BATCH].set(c).at[MOD_CTX_ROW].set(c_ctx)
    cos_t, sin_t = _rope_tables()
    big = {'ada_w': ada_w, 'w_in': w_in, 'w_out': w_out, 'w_gate': w_gate, 'w_up': w_up, 'w_down': w_down}
    for i in range(DEPTH):
        lp = {
            'ada_b': ada_b[i], 'norm1_g': norm1_g[i], 'norm2_g': norm2_g[i],
            'na_rpb': na_rpb[i], 'ret_log_decay': ret_log_decay[i],
            'ret_norm_g': ret_norm_g[i], 'sgu_ln_g': sgu_ln_g[i], 'sgu_ln_b': sgu_ln_b[i],
            'sgu_w': sgu_w[i], 'sgu_b': sgu_b[i], 'router_w': router_w[i],
        }
        h = _layer(h, c_rows, big, i, lp, cos_t, sin_t, update_ctx=(i < DEPTH - 1))
    return final_norm(h, final_g).reshape(BATCH, SEQ, D_MODEL)
```

```python
import functools

import numpy as np
import jax
import jax.numpy as jnp
from jax import lax
from jax.experimental import pallas as pl
from jax.experimental.pallas import tpu as pltpu

D_MODEL = 4096
BATCH = 2
SEQ = 4096
DEPTH = 2
GRID_W = 64
GRID_ROWS = SEQ // GRID_W
CTX_LEN = 256
HEAD_DIM = 128
NA_HEADS = 12
RET_HEADS = 12
SGU_GROUPS = 8
SGU_CHUNK = 128
RET_CHUNK = 128
WIN_ROWS = 8
WIN_COLS = 16
NA_WIDTH = NA_HEADS * HEAD_DIM
RET_WIDTH = RET_HEADS * HEAD_DIM
SGU_WIDTH = SGU_GROUPS * HEAD_DIM
MIX_WIDTH = NA_WIDTH + RET_WIDTH + SGU_WIDTH
IN_WIDTH = 3 * NA_WIDTH + 4 * RET_WIDTH + 2 * SGU_WIDTH
N_EXPERTS = 16
EXPERT_FF = 1024
EC_CAPACITY = 2
N_MOD = 6
ROPE_BASE = 10000.0
NORM_EPS = 1e-6

N_CTX = BATCH * CTX_LEN
N_LAT = BATCH * SEQ
N_TOK = N_CTX + N_LAT
MOD_CTX_ROW = BATCH
ATT_SCALE = HEAD_DIM ** -0.5
MASK_VALUE = -1e30

Q_ROWS = 4
Q_BLK = Q_ROWS * GRID_W
K_BAND_ROWS = 12
K_BAND = K_BAND_ROWS * GRID_W
N_QBLK = GRID_ROWS // Q_ROWS

VMEM_LIMIT = 56 * 1024 * 1024

F32 = jnp.float32
BF16 = jnp.bfloat16


def _params(n_axes, vmem=VMEM_LIMIT):
    return pltpu.CompilerParams(dimension_semantics=("arbitrary",) * n_axes, vmem_limit_bytes=vmem)


def _segment(tile_idx, tile_rows):
    return jnp.where(tile_idx < N_LAT // tile_rows, tile_idx // (SEQ // tile_rows), MOD_CTX_ROW)


def _silu(x):
    return x / (1.0 + jnp.exp(-x))


def _gelu_tanh(x):
    return x * (0.5 * (1.0 + jnp.tanh(np.float32(np.sqrt(2.0 / np.pi)) * (x + 0.044715 * (x * x * x)))))


ADA_TN = 512
ADA_ROWS = 16


def _adaln_kernel(c_ref, w_ref, b_ref, o_ref):
    s = _silu(c_ref[...]).astype(BF16)
    o_ref[...] = jnp.dot(s, w_ref[...].astype(BF16), preferred_element_type=F32) + b_ref[...]


def adaln(c_rows, ada_w, ada_b, li):
    n = ada_w.shape[2]
    return pl.pallas_call(
        _adaln_kernel,
        grid=(n // ADA_TN,),
        in_specs=[pl.BlockSpec((ADA_ROWS, D_MODEL), lambda j: (0, 0)),
                  pl.BlockSpec((None, D_MODEL, ADA_TN), lambda j: (li, 0, j)),
                  pl.BlockSpec((1, ADA_TN), lambda j: (0, j))],
        out_specs=pl.BlockSpec((ADA_ROWS, ADA_TN), lambda j: (0, j)),
        out_shape=jax.ShapeDtypeStruct((ADA_ROWS, n), F32),
        compiler_params=_params(1),
        name="adaln",
    )(c_rows, ada_w, ada_b.reshape(1, n))


NORM_TILE = 256


def _rms(x, g):
    return x * lax.rsqrt(jnp.mean(x * x, axis=-1, keepdims=True) + NORM_EPS) * g


def _norm_mod_kernel(x_ref, g_ref, sh_ref, sc_ref, o_ref):
    y = _rms(x_ref[...], g_ref[...])
    o_ref[...] = (y * (1.0 + sc_ref[0]) + sh_ref[0]).astype(o_ref.dtype)


def norm_mod(h, g, mod, shift_k, scale_k):
    def mod_map(k):
        return lambda i: (_segment(i, NORM_TILE) * N_MOD + k, 0, 0)

    return pl.pallas_call(
        _norm_mod_kernel,
        grid=(N_TOK // NORM_TILE,),
        in_specs=[pl.BlockSpec((NORM_TILE, D_MODEL), lambda i: (i, 0)),
                  pl.BlockSpec((1, D_MODEL), lambda i: (0, 0)),
                  pl.BlockSpec((1, 1, D_MODEL), mod_map(shift_k)),
                  pl.BlockSpec((1, 1, D_MODEL), mod_map(scale_k))],
        out_specs=pl.BlockSpec((NORM_TILE, D_MODEL), lambda i: (i, 0)),
        out_shape=jax.ShapeDtypeStruct((N_TOK, D_MODEL), BF16),
        compiler_params=_params(1),
        name="norm_mod",
    )(h, g.reshape(1, D_MODEL), mod, mod)


def _norm_mod_first_kernel(x_ref, c_ref, g_ref, sh_ref, sc_ref, o_ref, h_ref):
    src = jnp.where(pl.program_id(0) < N_LAT // NORM_TILE, x_ref[...], c_ref[...])
    h_ref[...] = src
    y = _rms(src, g_ref[...])
    o_ref[...] = (y * (1.0 + sc_ref[0]) + sh_ref[0]).astype(o_ref.dtype)


def norm_mod_first(x2d, ctx2d, g, mod, shift_k, scale_k):
    n_lat_tiles = N_LAT // NORM_TILE

    def mod_map(k):
        return lambda i: (_segment(i, NORM_TILE) * N_MOD + k, 0, 0)

    return pl.pallas_call(
        _norm_mod_first_kernel,
        grid=(N_TOK // NORM_TILE,),
        in_specs=[pl.BlockSpec((NORM_TILE, D_MODEL), lambda i: (jnp.minimum(i, n_lat_tiles - 1), 0)),
                  pl.BlockSpec((NORM_TILE, D_MODEL), lambda i: (jnp.maximum(i - n_lat_tiles, 0), 0)),
                  pl.BlockSpec((1, D_MODEL), lambda i: (0, 0)),
                  pl.BlockSpec((1, 1, D_MODEL), mod_map(shift_k)),
                  pl.BlockSpec((1, 1, D_MODEL), mod_map(scale_k))],
        out_specs=[pl.BlockSpec((NORM_TILE, D_MODEL), lambda i: (i, 0)),
                   pl.BlockSpec((NORM_TILE, D_MODEL), lambda i: (i, 0))],
        out_shape=[jax.ShapeDtypeStruct((N_TOK, D_MODEL), BF16),
                   jax.ShapeDtypeStruct((N_TOK, D_MODEL), F32)],
        compiler_params=_params(1),
        name="norm_mod_first",
    )(x2d, ctx2d, g.reshape(1, D_MODEL), mod, mod)


PROJ_TM = 2176
PROJ_TN = 256


def _proj_kernel(a_ref, w_ref, o_ref):
    o_ref[...] = jnp.dot(a_ref[...], w_ref[...].astype(BF16), preferred_element_type=F32)


def in_proj(a, w_in, li):
    return pl.pallas_call(
        _proj_kernel,
        grid=(N_TOK // PROJ_TM, IN_WIDTH // PROJ_TN),
        in_specs=[pl.BlockSpec((PROJ_TM, D_MODEL), lambda m, n: (m, 0), pipeline_mode=pl.Buffered(1)),
                  pl.BlockSpec((None, D_MODEL, PROJ_TN), lambda m, n: (li, 0, n))],
        out_specs=pl.BlockSpec((PROJ_TM, PROJ_TN), lambda m, n: (m, n)),
        out_shape=jax.ShapeDtypeStruct((N_TOK, IN_WIDTH), F32),
        compiler_params=_params(2),
        name="in_proj",
    )(a, w_in)


OUT_TN = 256


def _out_proj_kernel(oa_ref, ob_ref, oc_ref, w_ref, h_ref, g0_ref, g1_ref, gc_ref, o_ref):
    w = w_ref[...].astype(BF16)
    y = jnp.dot(oa_ref[...], w[0:NA_WIDTH], preferred_element_type=F32)
    y = y + jnp.dot(ob_ref[...], w[NA_WIDTH:NA_WIDTH + RET_WIDTH], preferred_element_type=F32)
    y = y + jnp.dot(oc_ref[...], w[NA_WIDTH + RET_WIDTH:MIX_WIDTH], preferred_element_type=F32)
    tm = o_ref.shape[0]
    row = pl.program_id(0) * tm + lax.broadcasted_iota(jnp.int32, (tm, 1), 0)
    gate = jnp.where(row < SEQ, g0_ref[0], jnp.where(row < N_LAT, g1_ref[0], gc_ref[0]))
    o_ref[...] = h_ref[...] + gate * y


def out_proj_residual(o_a, o_b, o_c, w_out, li, h, mod, gate_k, n_rows):
    tm = n_rows // 4

    def gate_map(seg):
        return lambda m, n: (seg * N_MOD + gate_k, 0, n)

    def row_tile(width):
        return pl.BlockSpec((tm, width), lambda m, n: (m, 0), pipeline_mode=pl.Buffered(1))

    gate_spec = [pl.BlockSpec((1, 1, OUT_TN), gate_map(seg)) for seg in (0, 1, MOD_CTX_ROW)]
    return pl.pallas_call(
        _out_proj_kernel,
        grid=(n_rows // tm, D_MODEL // OUT_TN),
        in_specs=[row_tile(NA_WIDTH), row_tile(RET_WIDTH), row_tile(SGU_WIDTH),
                  pl.BlockSpec((None, MIX_WIDTH, OUT_TN), lambda m, n: (li, 0, n)),
                  pl.BlockSpec((tm, OUT_TN), lambda m, n: (m, n))] + gate_spec,
        out_specs=pl.BlockSpec((tm, OUT_TN), lambda m, n: (m, n)),
        out_shape=jax.ShapeDtypeStruct((N_TOK, D_MODEL), F32),
        input_output_aliases={4: 0},
        compiler_params=_params(2),
        name="out_proj",
    )(o_a, o_b, o_c, w_out, h, mod, mod, mod)


def _na_kernel(q_ref, k0_ref, k1_ref, k2_ref, kc_ref, v0_ref, v1_ref, v2_ref, vc_ref, bias_ref, o_ref):
    for h in range(NA_HEADS):
        cols = slice(h * HEAD_DIM, (h + 1) * HEAD_DIM)
        q = (q_ref[:, cols] * ATT_SCALE).astype(BF16)

        def scores(k_ref):
            k = k_ref[:, cols].astype(BF16)
            return lax.dot_general(q, k, (((1,), (1,)), ((), ())), preferred_element_type=F32)

        s_win = jnp.concatenate([scores(k0_ref), scores(k1_ref), scores(k2_ref)], axis=1) + bias_ref[h]
        s_ctx = scores(kc_ref)
        m = jnp.maximum(jnp.max(s_win, axis=1, keepdims=True), jnp.max(s_ctx, axis=1, keepdims=True))
        e_win = jnp.exp(s_win - m)
        e_ctx = jnp.exp(s_ctx - m)
        denom = jnp.sum(e_win, axis=1, keepdims=True) + jnp.sum(e_ctx, axis=1, keepdims=True)
        acc = jnp.dot(e_ctx.astype(BF16), vc_ref[:, cols].astype(BF16), preferred_element_type=F32)
        for t, v_ref in enumerate((v0_ref, v1_ref, v2_ref)):
            e = e_win[:, t * Q_BLK:(t + 1) * Q_BLK].astype(BF16)
            acc = acc + jnp.dot(e, v_ref[:, cols].astype(BF16), preferred_element_type=F32)
        o_ref[:, cols] = (acc / denom).astype(o_ref.dtype)


def _bias_tile_kernel(off_ref, toep_ref, o_ref):
    base = pl.program_id(1) * (Q_ROWS * K_BAND_ROWS)
    for q in range(Q_ROWS):
        for k in range(0, K_BAND_ROWS, 2):
            left = toep_ref[off_ref[base + q * K_BAND_ROWS + k]]
            right = toep_ref[off_ref[base + q * K_BAND_ROWS + k + 1]]
            o_ref[q * GRID_W:(q + 1) * GRID_W, k * GRID_W:(k + 2) * GRID_W] = jnp.concatenate([left, right], axis=1)


def _na_bias_table(rpb):
    n_off = 2 * WIN_ROWS - 1
    pad = GRID_W - WIN_COLS
    ext = jnp.pad(rpb.astype(F32), ((0, 0), (0, 0), (pad, pad)))
    toep = jnp.stack([ext[..., GRID_W - 1 - qc:2 * GRID_W - 1 - qc] for qc in range(GRID_W)], axis=-2)
    qc = np.arange(GRID_W)
    c0 = np.clip(qc - WIN_COLS // 2, 0, GRID_W - WIN_COLS)
    ok_c = (qc[None, :] >= c0[:, None]) & (qc[None, :] < c0[:, None] + WIN_COLS)
    toep = jnp.where(ok_c, toep, MASK_VALUE)
    toep = jnp.concatenate([toep, jnp.full((NA_HEADS, 1, GRID_W, GRID_W), MASK_VALUE, F32)], axis=1)
    off = np.full((4, Q_ROWS, K_BAND_ROWS), n_off, np.int32)
    for pat, blk in enumerate((0, 1, N_QBLK - 1)):
        band0 = int(np.clip(Q_ROWS * blk - Q_ROWS, 0, GRID_ROWS - K_BAND_ROWS))
        for q in range(Q_ROWS):
            qr = Q_ROWS * blk + q
            r0 = int(np.clip(qr - WIN_ROWS // 2, 0, GRID_ROWS - WIN_ROWS))
            for k in range(K_BAND_ROWS):
                kr = band0 + k
                if r0 <= kr < r0 + WIN_ROWS:
                    off[pat, q, k] = kr - qr + (WIN_ROWS - 1)
    grid_spec = pltpu.PrefetchScalarGridSpec(
        num_scalar_prefetch=1,
        grid=(NA_HEADS, 4),
        in_specs=[pl.BlockSpec((None, n_off + 1, GRID_W, GRID_W), lambda h, p, off: (h, 0, 0, 0))],
        out_specs=pl.BlockSpec((None, None, Q_BLK, K_BAND), lambda h, p, off: (p, h, 0, 0)))
    return pl.pallas_call(
        _bias_tile_kernel,
        grid_spec=grid_spec,
        out_shape=jax.ShapeDtypeStruct((4, NA_HEADS, Q_BLK, K_BAND), F32),
        compiler_params=_params(2),
        name="na_bias_table",
    )(jnp.asarray(off.reshape(-1)), toep)


def na_attention(p, rpb, with_ctx):
    bias = _na_bias_table(rpb)
    lead = 1 if with_ctx else 0

    def lat_block(b, j):
        return b * N_QBLK + j

    def ctx_block(b):
        return N_LAT // Q_BLK + b

    def q_map(b, i):
        return (jnp.where(i < lead, ctx_block(b), lat_block(b, i - lead)), 0)

    def band_map(t, col):
        def f(b, i):
            return (lat_block(b, jnp.clip(i - lead - 1, 0, N_QBLK - 3)) + t, col)
        return f

    def ctx_map(col):
        return lambda b, i: (ctx_block(b), col)

    def bias_map(b, i):
        j = i - lead
        return (jnp.where(j < 0, 3, jnp.where(j == 0, 0, jnp.where(j == N_QBLK - 1, 2, 1))), 0, 0, 0)

    def out_map(b, i):
        return (jnp.where(i < lead, ctx_block(b), lat_block(b, i - lead)), 0)

    blk = (Q_BLK, NA_WIDTH)
    return pl.pallas_call(
        _na_kernel,
        grid=(BATCH, N_QBLK + lead),
        in_specs=[pl.BlockSpec(blk, q_map),
                  pl.BlockSpec(blk, band_map(0, 1)), pl.BlockSpec(blk, band_map(1, 1)),
                  pl.BlockSpec(blk, band_map(2, 1)), pl.BlockSpec(blk, ctx_map(1)),
                  pl.BlockSpec(blk, band_map(0, 2)), pl.BlockSpec(blk, band_map(1, 2)),
                  pl.BlockSpec(blk, band_map(2, 2)), pl.BlockSpec(blk, ctx_map(2)),
                  pl.BlockSpec((None, NA_HEADS, Q_BLK, K_BAND), bias_map, pipeline_mode=pl.Buffered(1))],
        out_specs=pl.BlockSpec(blk, out_map),
        out_shape=jax.ShapeDtypeStruct((N_TOK if with_ctx else N_LAT, NA_WIDTH), BF16),
        compiler_params=_params(2),
        name="na_attention",
    )(p, p, p, p, p, p, p, p, p, bias)


CTX_CHUNKS = CTX_LEN // RET_CHUNK
LAT_CHUNKS = SEQ // RET_CHUNK
RET_STEPS = CTX_CHUNKS + LAT_CHUNKS
RET_Q_COL, RET_K_COL, RET_V_COL, RET_G_COL = 3, 4, 5, 6


def _ret_kernel(ld_ref, q_ref, k_ref, v_ref, cos_ref, sin_ref, *rest, backward):
    if backward:
        of_ref, g_ref, ng_ref, o_ref, st_ref = rest
    else:
        o_ref, st_ref = rest

    @pl.when(pl.program_id(1) == 0)
    def _():
        st_ref[...] = jnp.zeros_like(st_ref)

    ii = lax.broadcasted_iota(jnp.int32, (RET_CHUNK, RET_CHUNK), 0)
    jj = lax.broadcasted_iota(jnp.int32, (RET_CHUNK, RET_CHUNK), 1)
    diff = ((jj - ii) if backward else (ii - jj)).astype(F32)
    causal = diff >= 0.0
    diff = jnp.maximum(diff, 0.0)
    row = lax.broadcasted_iota(jnp.int32, (RET_CHUNK, 1), 0)
    pos = ((RET_CHUNK - 1 - row) if backward else row).astype(F32)
    first_half = (jj % (HEAD_DIM // 2)) < (HEAD_DIM // 4)
    cos = cos_ref[...]
    sin = sin_ref[...]

    def rope(x):
        partner = jnp.where(first_half, pltpu.roll(x, HEAD_DIM - HEAD_DIM // 4, axis=1),
                            pltpu.roll(x, HEAD_DIM // 4, axis=1))
        return x * cos + partner * sin

    for hh in range(RET_HEADS):
        cols = slice(hh * HEAD_DIM, (hh + 1) * HEAD_DIM)
        ld = ld_ref[hh]
        inner_decay = jnp.where(causal, jnp.exp(diff * ld), 0.0)
        q_decay = jnp.exp((pos + 1.0) * ld)
        k_decay = jnp.exp((RET_CHUNK - 1.0 - pos) * ld)
        chunk_decay = jnp.exp(jnp.full((1, HEAD_DIM), RET_CHUNK, F32) * ld)
        q = rope(q_ref[:, cols])
        k = rope(k_ref[:, cols]) * ATT_SCALE
        qb = q.astype(BF16)
        vb = v_ref[:, cols].astype(BF16)
        att = lax.dot_general(qb, k.astype(BF16), (((1,), (1,)), ((), ())), preferred_element_type=F32)
        inner = jnp.dot((att * inner_decay).astype(BF16), vb, preferred_element_type=F32)
        state = st_ref[hh]
        cross = jnp.dot(qb, state.astype(BF16), preferred_element_type=F32) * q_decay
        kd_t = (k * k_decay).T.astype(BF16)
        st_ref[hh] = state * chunk_decay + jnp.dot(kd_t, vb, preferred_element_type=F32)
        o = inner + cross
        if backward:
            tot = of_ref[:, cols] + o
            y = _rms(tot, ng_ref[:, cols]) * _silu(g_ref[:, cols])
            o_ref[:, cols] = y.astype(o_ref.dtype)
        else:
            o_ref[:, cols] = o


def _ret_chunk_block(b, s, backward):
    ctx0 = N_LAT // RET_CHUNK + b * CTX_CHUNKS
    lat0 = b * LAT_CHUNKS
    if backward:
        ctx_blk = ctx0 + (CTX_CHUNKS - 1 - s)
        lat_blk = lat0 + (LAT_CHUNKS - 1 - (s - CTX_CHUNKS))
    else:
        ctx_blk = ctx0 + s
        lat_blk = lat0 + (s - CTX_CHUNKS)
    return jnp.where(s < CTX_CHUNKS, ctx_blk, lat_blk)


def retention_pass(p, log_decay, cos_t, sin_t, backward, o_fwd=None, norm_g=None):
    def col_map(col):
        return lambda b, s: (_ret_chunk_block(b, s, backward), col)

    wide = (RET_CHUNK, RET_WIDTH)
    in_specs = [pl.BlockSpec(memory_space=pltpu.SMEM),
                pl.BlockSpec(wide, col_map(RET_Q_COL)), pl.BlockSpec(wide, col_map(RET_K_COL)),
                pl.BlockSpec(wide, col_map(RET_V_COL)),
                pl.BlockSpec((RET_CHUNK, HEAD_DIM), col_map(0)), pl.BlockSpec((RET_CHUNK, HEAD_DIM), col_map(0))]
    args = [log_decay, p, p, p, cos_t, sin_t]
    if backward:
        in_specs += [pl.BlockSpec(wide, col_map(0)), pl.BlockSpec(wide, col_map(RET_G_COL)),
                     pl.BlockSpec((1, RET_WIDTH), lambda b, s: (0, 0))]
        args += [o_fwd, p, norm_g.reshape(1, RET_WIDTH)]
    return pl.pallas_call(
        functools.partial(_ret_kernel, backward=backward),
        grid=(BATCH, RET_STEPS),
        in_specs=in_specs,
        out_specs=pl.BlockSpec(wide, col_map(0)),
        out_shape=jax.ShapeDtypeStruct((N_TOK, RET_WIDTH), BF16 if backward else F32),
        scratch_shapes=[pltpu.VMEM((RET_HEADS, HEAD_DIM, HEAD_DIM), F32)],
        compiler_params=_params(2),
        name="retention_bwd" if backward else "retention_fwd",
    )(*args)


def _rope_tables():
    quarter = HEAD_DIM // 4
    inv_freq = np.float64(ROPE_BASE) ** (-np.arange(quarter, dtype=np.float64) / quarter)
    t = np.arange(SEQ)
    ang_r = (t // GRID_W)[:, None] * inv_freq[None, :]
    ang_c = (t % GRID_W)[:, None] * inv_freq[None, :]
    cos = np.concatenate([np.cos(ang_r), np.cos(ang_r), np.cos(ang_c), np.cos(ang_c)], axis=-1)
    sin = np.concatenate([-np.sin(ang_r), np.sin(ang_r), -np.sin(ang_c), np.sin(ang_c)], axis=-1)
    cos = np.concatenate([cos] * BATCH + [np.ones((N_CTX, HEAD_DIM))], axis=0).astype(np.float32)
    sin = np.concatenate([sin] * BATCH + [np.zeros((N_CTX, HEAD_DIM))], axis=0).astype(np.float32)
    return jnp.asarray(cos), jnp.asarray(sin)


SGU_TOK = 512
SGU_COLS = 512
SGU_U_COL = (3 * NA_WIDTH + 4 * RET_WIDTH) // SGU_COLS
SGU_V_COL = SGU_U_COL + SGU_WIDTH // SGU_COLS
SGU_GPB = SGU_COLS // HEAD_DIM


def _sgu_kernel(u_ref, v_ref, lng_ref, lnb_ref, w_ref, bs_ref, o_ref):
    for g in range(SGU_GPB):
        cols = slice(g * HEAD_DIM, (g + 1) * HEAD_DIM)
        w = w_ref[g].astype(BF16)
        lng = lng_ref[:, cols]
        lnb = lnb_ref[:, cols]
        bias = bs_ref[:, g:g + 1]
        for c in range(SGU_TOK // SGU_CHUNK):
            rows = slice(c * SGU_CHUNK, (c + 1) * SGU_CHUNK)
            v = _gelu_tanh(v_ref[rows, cols])
            mu = jnp.mean(v, axis=-1, keepdims=True)
            vc = v - mu
            var = jnp.mean(vc * vc, axis=-1, keepdims=True)
            vn = vc * lax.rsqrt(var + NORM_EPS) * lng + lnb
            s = jnp.dot(w, vn.astype(BF16), preferred_element_type=F32) + bias
            o_ref[rows, cols] = (_gelu_tanh(u_ref[rows, cols]) * s).astype(o_ref.dtype)


def spatial_gating(p, ln_g, ln_b, w_s, b_s, n_rows):
    n_tiles = n_rows // SGU_TOK
    n_half = SGU_WIDTH // SGU_COLS
    bs_t = b_s.reshape(n_half, SGU_GPB, SGU_CHUNK).transpose(0, 2, 1)
    return pl.pallas_call(
        _sgu_kernel,
        grid=(n_tiles, n_half),
        in_specs=[pl.BlockSpec((SGU_TOK, SGU_COLS), lambda t, c: (t, SGU_U_COL + c)),
                  pl.BlockSpec((SGU_TOK, SGU_COLS), lambda t, c: (t, SGU_V_COL + c)),
                  pl.BlockSpec((1, SGU_COLS), lambda t, c: (0, c)),
                  pl.BlockSpec((1, SGU_COLS), lambda t, c: (0, c)),
                  pl.BlockSpec((SGU_GPB, SGU_CHUNK, SGU_CHUNK), lambda t, c: (c, 0, 0)),
                  pl.BlockSpec((None, SGU_CHUNK, SGU_GPB), lambda t, c: (c, 0, 0))],
        out_specs=pl.BlockSpec((SGU_TOK, SGU_COLS), lambda t, c: (t, c)),
        out_shape=jax.ShapeDtypeStruct((n_rows, SGU_WIDTH), BF16),
        compiler_params=_params(2),
        name="spatial_gating",
    )(p, p, ln_g.reshape(1, SGU_WIDTH), ln_b.reshape(1, SGU_WIDTH), w_s, bs_t)


ROUTER_LANES = 128


def _norm_router_kernel(x_ref, g_ref, sh_ref, sc_ref, rw_ref, f_ref, aff_ref):
    y = _rms(x_ref[...], g_ref[...])
    f = y * (1.0 + sc_ref[0]) + sh_ref[0]
    fb = f.astype(BF16)
    f_ref[...] = fb
    logits = jnp.dot(fb, rw_ref[...].astype(BF16), preferred_element_type=F32)
    lane = lax.broadcasted_iota(jnp.int32, logits.shape, 1)
    logits = jnp.where(lane < N_EXPERTS, logits, MASK_VALUE)
    e = jnp.exp(logits - jnp.max(logits, axis=1, keepdims=True))
    aff_ref[...] = e / jnp.sum(e, axis=1, keepdims=True)


def norm_router(h, g, mod, shift_k, scale_k, router_w, n_rows):
    n_tiles = n_rows // NORM_TILE
    rw = jnp.pad(router_w, ((0, 0), (0, ROUTER_LANES - N_EXPERTS)))

    def mod_map(k):
        return lambda i: (_segment(i, NORM_TILE) * N_MOD + k, 0, 0)

    return pl.pallas_call(
        _norm_router_kernel,
        grid=(n_tiles,),
        in_specs=[pl.BlockSpec((NORM_TILE, D_MODEL), lambda i: (i, 0)),
                  pl.BlockSpec((1, D_MODEL), lambda i: (0, 0)),
                  pl.BlockSpec((1, 1, D_MODEL), mod_map(shift_k)),
                  pl.BlockSpec((1, 1, D_MODEL), mod_map(scale_k)),
                  pl.BlockSpec((D_MODEL, ROUTER_LANES), lambda i: (0, 0))],
        out_specs=[pl.BlockSpec((NORM_TILE, D_MODEL), lambda i: (i, 0)),
                   pl.BlockSpec((NORM_TILE, ROUTER_LANES), lambda i: (i, 0))],
        out_shape=[jax.ShapeDtypeStruct((n_rows, D_MODEL), BF16),
                   jax.ShapeDtypeStruct((n_rows, ROUTER_LANES), F32)],
        compiler_params=_params(1),
        name="norm_router",
    )(h, g.reshape(1, D_MODEL), mod, mod, rw)


FF_CHUNK = 256
DOWN_TN = 1024


def _ffn_up_kernel(x_ref, wg_ref, wu_ref, o_ref):
    x = x_ref[...]
    gate = jnp.dot(x, wg_ref[...].astype(BF16), preferred_element_type=F32)
    up = jnp.dot(x, wu_ref[...].astype(BF16), preferred_element_type=F32)
    o_ref[...] = (_silu(gate) * up).astype(o_ref.dtype)


def _ffn_down_kernel(h_ref, wd_ref, o_ref):
    o_ref[...] = jnp.dot(h_ref[...], wd_ref[...].astype(BF16), preferred_element_type=F32)


def expert_ffn(xin, w_gate, w_up, w_down, li):
    r = xin.shape[1]
    hid = pl.pallas_call(
        _ffn_up_kernel,
        grid=(N_EXPERTS, EXPERT_FF // FF_CHUNK),
        in_specs=[pl.BlockSpec((None, r, D_MODEL), lambda e, f: (e, 0, 0)),
                  pl.BlockSpec((None, None, D_MODEL, FF_CHUNK), lambda e, f: (li, e, 0, f)),
                  pl.BlockSpec((None, None, D_MODEL, FF_CHUNK), lambda e, f: (li, e, 0, f))],
        out_specs=pl.BlockSpec((None, r, FF_CHUNK), lambda e, f: (e, 0, f)),
        out_shape=jax.ShapeDtypeStruct((N_EXPERTS, r, EXPERT_FF), BF16),
        compiler_params=_params(2),
        name="ffn_up",
    )(xin, w_gate, w_up)
    return pl.pallas_call(
        _ffn_down_kernel,
        grid=(N_EXPERTS, D_MODEL // DOWN_TN),
        in_specs=[pl.BlockSpec((None, r, EXPERT_FF), lambda e, n: (e, 0, 0)),
                  pl.BlockSpec((None, None, EXPERT_FF, DOWN_TN), lambda e, n: (li, e, 0, n))],
        out_specs=pl.BlockSpec((None, r, DOWN_TN), lambda e, n: (e, 0, n)),
        out_shape=jax.ShapeDtypeStruct((N_EXPERTS, r, D_MODEL), F32),
        compiler_params=_params(2),
        name="ffn_down",
    )(hid, w_down)


COMBINE_CHUNK = 256
COMBINE_UNROLL = 8


def _combine_kernel(rows_ref, y_ref, gate_ref, gf_ref, h_in_ref, h_ref, buf_ref, gsem, ssem, *, chunk, n_chunks):
    del h_in_ref
    e = pl.program_id(0)
    c = pl.program_id(1)
    n_e = pl.num_programs(0)
    slot = c % 2

    def copy(hbm, vmem, s, gather):
        return (pltpu.make_async_copy(hbm, vmem, gsem.at[s]) if gather
                else pltpu.make_async_copy(vmem, hbm, ssem.at[s]))

    def start_all(s, ee, cc, gather):
        base = (ee * n_chunks + cc) * chunk

        def body(g, carry):
            for u in range(COMBINE_UNROLL):
                j = g * COMBINE_UNROLL + u
                row = rows_ref[base + j]
                copy(h_ref.at[pl.ds(row, 1)], buf_ref.at[s, pl.ds(j, 1)], s, gather).start(priority=u % 2)
            return carry
        lax.fori_loop(0, chunk // COMBINE_UNROLL, body, 0)

    def wait_all(s, gather):
        copy(h_ref.at[pl.ds(0, chunk)], buf_ref.at[s], s, gather).wait()

    @pl.when((e == 0) & (c == 0))
    def _():
        start_all(0, 0, 0, True)

    wait_all(slot, True)
    buf_ref[slot] = buf_ref[slot] + gf_ref[0] * (gate_ref[...] * y_ref[...])

    @pl.when(c > 0)
    def _():
        wait_all(1 - slot, False)

    @pl.when(c < n_chunks - 1)
    def _():
        start_all(1 - slot, e, c + 1, True)
        start_all(slot, e, c, False)

    @pl.when(c == n_chunks - 1)
    def _():
        start_all(slot, e, c, False)
        wait_all(slot, False)

        @pl.when(e < n_e - 1)
        def _():
            start_all(0, e + 1, 0, True)


def moe_combine(h, y, gates, rows, mod, gate_k, seg_of_chunk, chunk, row_off):
    n_e, r = rows.shape
    n_chunks = r // chunk
    blk0 = row_off // chunk
    d = h.shape[1]
    grid_spec = pltpu.PrefetchScalarGridSpec(
        num_scalar_prefetch=1,
        grid=(n_e, n_chunks),
        in_specs=[pl.BlockSpec((None, chunk, d), lambda e, c, rows: (e, blk0 + c, 0)),
                  pl.BlockSpec((None, chunk, 1), lambda e, c, rows: (e, c, 0)),
                  pl.BlockSpec((1, 1, d), lambda e, c, rows: (seg_of_chunk(c) * N_MOD + gate_k, 0, 0)),
                  pl.BlockSpec(memory_space=pl.ANY)],
        out_specs=pl.BlockSpec(memory_space=pl.ANY),
        scratch_shapes=[pltpu.VMEM((2, chunk, d), F32),
                        pltpu.SemaphoreType.DMA((2,)), pltpu.SemaphoreType.DMA((2,))])
    return pl.pallas_call(
        functools.partial(_combine_kernel, chunk=chunk, n_chunks=n_chunks),
        grid_spec=grid_spec,
        out_shape=jax.ShapeDtypeStruct(h.shape, h.dtype),
        input_output_aliases={4: 0},
        compiler_params=_params(2),
        name="moe_combine",
    )(rows.reshape(-1), y, gates.reshape(n_e, r, 1), mod, h)


def _final_norm_kernel(x_ref, g_ref, o_ref):
    o_ref[...] = _rms(x_ref[...], g_ref[...])


def final_norm(h, g):
    return pl.pallas_call(
        _final_norm_kernel,
        grid=(N_LAT // NORM_TILE,),
        in_specs=[pl.BlockSpec((NORM_TILE, D_MODEL), lambda i: (i, 0)),
                  pl.BlockSpec((1, D_MODEL), lambda i: (0, 0))],
        out_specs=pl.BlockSpec((NORM_TILE, D_MODEL), lambda i: (i, 0)),
        out_shape=jax.ShapeDtypeStruct((N_LAT, D_MODEL), F32),
        compiler_params=_params(1),
        name="final_norm",
    )(h, g.reshape(1, D_MODEL))


def _route(aff, n_per_sample, row0):
    cap = EC_CAPACITY * n_per_sample // N_EXPERTS
    a = aff[row0:row0 + BATCH * n_per_sample, :N_EXPERTS].reshape(BATCH, n_per_sample, N_EXPERTS)
    gates, idx = lax.top_k(a.transpose(0, 2, 1), cap)
    rows = idx + (row0 + jnp.arange(BATCH) * n_per_sample)[:, None, None]
    return rows, gates


def _moe(h, f, aff, mod, big, li, update_ctx):
    def per_expert(t):
        return t.transpose(1, 0, 2).reshape(N_EXPERTS, -1)

    rows, gates = _route(aff, SEQ, 0)
    chunks_per_sample = rows.shape[2] // COMBINE_CHUNK
    parts = [(per_expert(rows), per_expert(gates), lambda c: c // chunks_per_sample, COMBINE_CHUNK)]
    if update_ctx:
        rows_c, gates_c = _route(aff, CTX_LEN, N_LAT)
        parts.append((per_expert(rows_c), per_expert(gates_c), lambda c: MOD_CTX_ROW, BATCH * rows_c.shape[2]))
    all_rows = jnp.concatenate([part[0] for part in parts], axis=1)
    y = expert_ffn(f[all_rows], big['w_gate'], big['w_up'], big['w_down'], li)
    off = 0
    for part_rows, part_gates, seg_of_chunk, chunk in parts:
        h = moe_combine(h, y, part_gates, part_rows, mod, 5, seg_of_chunk, chunk, off)
        off += part_rows.shape[1]
    return h


def _layer(h, c_rows, big, li, lp, cos_t, sin_t, update_ctx):
    n_rows = N_TOK if update_ctx else N_LAT
    mod = adaln(c_rows, big['ada_w'], lp['ada_b'], li)[:3].reshape(3 * N_MOD, 1, D_MODEL)
    if isinstance(h, tuple):
        a, h = norm_mod_first(h[0], h[1], lp['norm1_g'], mod, 0, 1)
    else:
        a = norm_mod(h, lp['norm1_g'], mod, 0, 1)
    p = in_proj(a, big['w_in'], li)
    o_a = na_attention(p, lp['na_rpb'], with_ctx=update_ctx)
    o_f = retention_pass(p, lp['ret_log_decay'][0], cos_t, sin_t, backward=False)
    o_b = retention_pass(p, lp['ret_log_decay'][1], cos_t, sin_t, backward=True, o_fwd=o_f,
                         norm_g=lp['ret_norm_g'])
    o_c = spatial_gating(p, lp['sgu_ln_g'], lp['sgu_ln_b'], lp['sgu_w'], lp['sgu_b'], n_rows)
    h = out_proj_residual(o_a, o_b, o_c, big['w_out'], li, h, mod, 2, n_rows)
    f, aff = norm_router(h, lp['norm2_g'], mod, 3, 4, lp['router_w'], n_rows)
    return _moe(h, f, aff, mod, big, li, update_ctx)


def kernel(x, c, ctx, c_ctx, ada_w, ada_b, norm1_g, norm2_g, w_in, w_out, na_rpb, ret_log_decay, ret_norm_g,
           sgu_ln_g, sgu_ln_b, sgu_w, sgu_b, router_w, w_gate, w_up, w_down, final_g):
    h = (x.reshape(N_LAT, D_MODEL), ctx.reshape(N_CTX, D_MODEL))
    c_rows = jnp.zeros((ADA_ROWS, D_MODEL), F32).at[:BATCH].set(c).at[MOD_CTX_ROW].set(c_ctx)
    cos_t, sin_t = _rope_tables()
    big = {'ada_w': ada_w, 'w_in': w_in, 'w_out': w_out, 'w_gate': w_gate, 'w_up': w_up, 'w_down': w_down}
    for i in range(DEPTH):
        lp = {
            'ada_b': ada_b[i], 'norm1_g': norm1_g[i], 'norm2_g': norm2_g[i],
            'na_rpb': na_rpb[i], 'ret_log_decay': ret_log_decay[i],
            'ret_norm_g': ret_norm_g[i], 'sgu_ln_g': sgu_ln_g[i], 'sgu_ln_b': sgu_ln_b[i],
            'sgu_w': sgu_w[i], 'sgu_b': sgu_b[i], 'router_w': router_w[i],
        }
        h = _layer(h, c_rows, big, i, lp, cos_t, sin_t, update_ctx=(i < DEPTH - 1))
    return final_norm(h, final_g).reshape(BATCH, SEQ, D_MODEL)
```

```python
import functools

import numpy as np
import jax
import jax.numpy as jnp
from jax import lax
from jax.experimental import pallas as pl
from jax.experimental.pallas import tpu as pltpu

D_MODEL = 4096
BATCH = 2
SEQ = 4096
DEPTH = 2
GRID_W = 64
GRID_ROWS = SEQ // GRID_W
CTX_LEN = 256
HEAD_DIM = 128
NA_HEADS = 12
RET_HEADS = 12
SGU_GROUPS = 8
SGU_CHUNK = 128
RET_CHUNK = 128
WIN_ROWS = 8
WIN_COLS = 16
NA_WIDTH = NA_HEADS * HEAD_DIM
RET_WIDTH = RET_HEADS * HEAD_DIM
SGU_WIDTH = SGU_GROUPS * HEAD_DIM
MIX_WIDTH = NA_WIDTH + RET_WIDTH + SGU_WIDTH
IN_WIDTH = 3 * NA_WIDTH + 4 * RET_WIDTH + 2 * SGU_WIDTH
N_EXPERTS = 16
EXPERT_FF = 1024
EC_CAPACITY = 2
N_MOD = 6
ROPE_BASE = 10000.0
NORM_EPS = 1e-6

N_CTX = BATCH * CTX_LEN
N_LAT = BATCH * SEQ
N_TOK = N_CTX + N_LAT
MOD_CTX_ROW = BATCH
ATT_SCALE = HEAD_DIM ** -0.5
MASK_VALUE = -1e30

Q_ROWS = 4
Q_BLK = Q_ROWS * GRID_W
K_BAND_ROWS = 12
K_BAND = K_BAND_ROWS * GRID_W
N_QBLK = GRID_ROWS // Q_ROWS

VMEM_LIMIT = 56 * 1024 * 1024

F32 = jnp.float32
BF16 = jnp.bfloat16


def _params(n_axes, vmem=VMEM_LIMIT):
    return pltpu.CompilerParams(dimension_semantics=("arbitrary",) * n_axes, vmem_limit_bytes=vmem)


def _segment(tile_idx, tile_rows):
    return jnp.where(tile_idx < N_LAT // tile_rows, tile_idx // (SEQ // tile_rows), MOD_CTX_ROW)


def _silu(x):
    return x / (1.0 + jnp.exp(-x))


def _gelu_tanh(x):
    return x * (0.5 * (1.0 + jnp.tanh(np.float32(np.sqrt(2.0 / np.pi)) * (x + 0.044715 * (x * x * x)))))


ADA_TN = 512
ADA_ROWS = 16


def _adaln_kernel(c_ref, w_ref, b_ref, o_ref):
    s = _silu(c_ref[...]).astype(BF16)
    o_ref[...] = jnp.dot(s, w_ref[...].astype(BF16), preferred_element_type=F32) + b_ref[...]


def adaln(c_rows, ada_w, ada_b, li):
    n = ada_w.shape[2]
    return pl.pallas_call(
        _adaln_kernel,
        grid=(n // ADA_TN,),
        in_specs=[pl.BlockSpec((ADA_ROWS, D_MODEL), lambda j: (0, 0)),
                  pl.BlockSpec((None, D_MODEL, ADA_TN), lambda j: (li, 0, j)),
                  pl.BlockSpec((1, ADA_TN), lambda j: (0, j))],
        out_specs=pl.BlockSpec((ADA_ROWS, ADA_TN), lambda j: (0, j)),
        out_shape=jax.ShapeDtypeStruct((ADA_ROWS, n), F32),
        compiler_params=_params(1),
        name="adaln",
    )(c_rows, ada_w, ada_b.reshape(1, n))


NORM_TILE = 256


def _rms(x, g):
    return x * lax.rsqrt(jnp.mean(x * x, axis=-1, keepdims=True) + NORM_EPS) * g


def _norm_mod_kernel(x_ref, g_ref, sh_ref, sc_ref, o_ref):
    y = _rms(x_ref[...], g_ref[...])
    o_ref[...] = (y * (1.0 + sc_ref[0]) + sh_ref[0]).astype(o_ref.dtype)


def norm_mod(h, g, mod, shift_k, scale_k):
    def mod_map(k):
        return lambda i: (_segment(i, NORM_TILE) * N_MOD + k, 0, 0)

    return pl.pallas_call(
        _norm_mod_kernel,
        grid=(N_TOK // NORM_TILE,),
        in_specs=[pl.BlockSpec((NORM_TILE, D_MODEL), lambda i: (i, 0)),
                  pl.BlockSpec((1, D_MODEL), lambda i: (0, 0)),
                  pl.BlockSpec((1, 1, D_MODEL), mod_map(shift_k)),
                  pl.BlockSpec((1, 1, D_MODEL), mod_map(scale_k))],
        out_specs=pl.BlockSpec((NORM_TILE, D_MODEL), lambda i: (i, 0)),
        out_shape=jax.ShapeDtypeStruct((N_TOK, D_MODEL), BF16),
        compiler_params=_params(1),
        name="norm_mod",
    )(h, g.reshape(1, D_MODEL), mod, mod)


def _norm_mod_first_kernel(x_ref, c_ref, g_ref, sh_ref, sc_ref, o_ref, h_ref):
    src = jnp.where(pl.program_id(0) < N_LAT // NORM_TILE, x_ref[...], c_ref[...])
    h_ref[...] = src
    y = _rms(src, g_ref[...])
    o_ref[...] = (y * (1.0 + sc_ref[0]) + sh_ref[0]).astype(o_ref.dtype)


def norm_mod_first(x2d, ctx2d, g, mod, shift_k, scale_k):
    n_lat_tiles = N_LAT // NORM_TILE

    def mod_map(k):
        return lambda i: (_segment(i, NORM_TILE) * N_MOD + k, 0, 0)

    return pl.pallas_call(
        _norm_mod_first_kernel,
        grid=(N_TOK // NORM_TILE,),
        in_specs=[pl.BlockSpec((NORM_TILE, D_MODEL), lambda i: (jnp.minimum(i, n_lat_tiles - 1), 0)),
                  pl.BlockSpec((NORM_TILE, D_MODEL), lambda i: (jnp.maximum(i - n_lat_tiles, 0), 0)),
                  pl.BlockSpec((1, D_MODEL), lambda i: (0, 0)),
                  pl.BlockSpec((1, 1, D_MODEL), mod_map(shift_k)),
                  pl.BlockSpec((1, 1, D_MODEL), mod_map(scale_k))],
        out_specs=[pl.BlockSpec((NORM_TILE, D_MODEL), lambda i: (i, 0)),
                   pl.BlockSpec((NORM_TILE, D_MODEL), lambda i: (i, 0))],
        out_shape=[jax.ShapeDtypeStruct((N_TOK, D_MODEL), BF16),
                   jax.ShapeDtypeStruct((N_TOK, D_MODEL), F32)],
        compiler_params=_params(1),
        name="norm_mod_first",
    )(x2d, ctx2d, g.reshape(1, D_MODEL), mod, mod)


PROJ_TM = 2176
PROJ_TN = 256


def _proj_kernel(a_ref, w_ref, o_ref):
    o_ref[...] = jnp.dot(a_ref[...], w_ref[...].astype(BF16), preferred_element_type=F32)


def in_proj(a, w_in, li):
    return pl.pallas_call(
        _proj_kernel,
        grid=(N_TOK // PROJ_TM, IN_WIDTH // PROJ_TN),
        in_specs=[pl.BlockSpec((PROJ_TM, D_MODEL), lambda m, n: (m, 0), pipeline_mode=pl.Buffered(1)),
                  pl.BlockSpec((None, D_MODEL, PROJ_TN), lambda m, n: (li, 0, n))],
        out_specs=pl.BlockSpec((PROJ_TM, PROJ_TN), lambda m, n: (m, n)),
        out_shape=jax.ShapeDtypeStruct((N_TOK, IN_WIDTH), F32),
        compiler_params=_params(2),
        name="in_proj",
    )(a, w_in)


OUT_TN = 256


def _out_proj_kernel(oa_ref, ob_ref, oc_ref, w_ref, h_ref, g0_ref, g1_ref, gc_ref, o_ref):
    w = w_ref[...].astype(BF16)
    y = jnp.dot(oa_ref[...], w[0:NA_WIDTH], preferred_element_type=F32)
    y = y + jnp.dot(ob_ref[...], w[NA_WIDTH:NA_WIDTH + RET_WIDTH], preferred_element_type=F32)
    y = y + jnp.dot(oc_ref[...], w[NA_WIDTH + RET_WIDTH:MIX_WIDTH], preferred_element_type=F32)
    tm = o_ref.shape[0]
    row = pl.program_id(0) * tm + lax.broadcasted_iota(jnp.int32, (tm, 1), 0)
    gate = jnp.where(row < SEQ, g0_ref[0], jnp.where(row < N_LAT, g1_ref[0], gc_ref[0]))
    o_ref[...] = h_ref[...] + gate * y


def out_proj_residual(o_a, o_b, o_c, w_out, li, h, mod, gate_k, n_rows):
    tm = n_rows // 4

    def gate_map(seg):
        return lambda m, n: (seg * N_MOD + gate_k, 0, n)

    def row_tile(width):
        return pl.BlockSpec((tm, width), lambda m, n: (m, 0), pipeline_mode=pl.Buffered(1))

    gate_spec = [pl.BlockSpec((1, 1, OUT_TN), gate_map(seg)) for seg in (0, 1, MOD_CTX_ROW)]
    return pl.pallas_call(
        _out_proj_kernel,
        grid=(n_rows // tm, D_MODEL // OUT_TN),
        in_specs=[row_tile(NA_WIDTH), row_tile(RET_WIDTH), row_tile(SGU_WIDTH),
                  pl.BlockSpec((None, MIX_WIDTH, OUT_TN), lambda m, n: (li, 0, n)),
                  pl.BlockSpec((tm, OUT_TN), lambda m, n: (m, n))] + gate_spec,
        out_specs=pl.BlockSpec((tm, OUT_TN), lambda m, n: (m, n)),
        out_shape=jax.ShapeDtypeStruct((N_TOK, D_MODEL), F32),
        input_output_aliases={4: 0},
        compiler_params=_params(2),
        name="out_proj",
    )(o_a, o_b, o_c, w_out, h, mod, mod, mod)


def _na_kernel(q_ref, k0_ref, k1_ref, k2_ref, kc_ref, v0_ref, v1_ref, v2_ref, vc_ref, bias_ref, o_ref):
    for h in range(NA_HEADS):
        cols = slice(h * HEAD_DIM, (h + 1) * HEAD_DIM)
        q = (q_ref[:, cols] * ATT_SCALE).astype(BF16)

        def scores(k_ref):
            k = k_ref[:, cols].astype(BF16)
            return lax.dot_general(q, k, (((1,), (1,)), ((), ())), preferred_element_type=F32)

        s_win = jnp.concatenate([scores(k0_ref), scores(k1_ref), scores(k2_ref)], axis=1) + bias_ref[h]
        s_ctx = scores(kc_ref)
        m = jnp.maximum(jnp.max(s_win, axis=1, keepdims=True), jnp.max(s_ctx, axis=1, keepdims=True))
        e_win = jnp.exp(s_win - m)
        e_ctx = jnp.exp(s_ctx - m)
        denom = jnp.sum(e_win, axis=1, keepdims=True) + jnp.sum(e_ctx, axis=1, keepdims=True)
        acc = jnp.dot(e_ctx.astype(BF16), vc_ref[:, cols].astype(BF16), preferred_element_type=F32)
        for t, v_ref in enumerate((v0_ref, v1_ref, v2_ref)):
            e = e_win[:, t * Q_BLK:(t + 1) * Q_BLK].astype(BF16)
            acc = acc + jnp.dot(e, v_ref[:, cols].astype(BF16), preferred_element_type=F32)
        o_ref[:, cols] = (acc / denom).astype(o_ref.dtype)


def _bias_tile_kernel(off_ref, toep_ref, o_ref):
    base = pl.program_id(1) * (Q_ROWS * K_BAND_ROWS)
    for q in range(Q_ROWS):
        for k in range(0, K_BAND_ROWS, 2):
            left = toep_ref[off_ref[base + q * K_BAND_ROWS + k]]
            right = toep_ref[off_ref[base + q * K_BAND_ROWS + k + 1]]
            o_ref[q * GRID_W:(q + 1) * GRID_W, k * GRID_W:(k + 2) * GRID_W] = jnp.concatenate([left, right], axis=1)


def _na_bias_table(rpb):
    n_off = 2 * WIN_ROWS - 1
    n_rel = 2 * WIN_COLS - 1
    qc = np.arange(GRID_W)
    rel = qc[None, :] - qc[:, None] + (WIN_COLS - 1)
    c0 = np.clip(qc - WIN_COLS // 2, 0, GRID_W - WIN_COLS)
    ok_c = (qc[None, :] >= c0[:, None]) & (qc[None, :] < c0[:, None] + WIN_COLS)
    onehot = ((rel[None] == np.arange(n_rel)[:, None, None]) & ok_c[None]).astype(np.float32).reshape(n_rel, -1)
    toep = jnp.dot(rpb.astype(F32).reshape(-1, n_rel), onehot, precision=lax.Precision.HIGHEST)
    toep = jnp.where(ok_c.reshape(-1), toep, MASK_VALUE).reshape(NA_HEADS, n_off, GRID_W, GRID_W)
    toep = jnp.pad(toep, ((0, 0), (0, 1), (0, 0), (0, 0)), constant_values=MASK_VALUE)
    off = np.full((4, Q_ROWS, K_BAND_ROWS), n_off, np.int32)
    for pat, blk in enumerate((0, 1, N_QBLK - 1)):
        band0 = int(np.clip(Q_ROWS * blk - Q_ROWS, 0, GRID_ROWS - K_BAND_ROWS))
        for q in range(Q_ROWS):
            qr = Q_ROWS * blk + q
            r0 = int(np.clip(qr - WIN_ROWS // 2, 0, GRID_ROWS - WIN_ROWS))
            for k in range(K_BAND_ROWS):
                kr = band0 + k
                if r0 <= kr < r0 + WIN_ROWS:
                    off[pat, q, k] = kr - qr + (WIN_ROWS - 1)
    grid_spec = pltpu.PrefetchScalarGridSpec(
        num_scalar_prefetch=1,
        grid=(NA_HEADS, 4),
        in_specs=[pl.BlockSpec((None, n_off + 1, GRID_W, GRID_W), lambda h, p, off: (h, 0, 0, 0))],
        out_specs=pl.BlockSpec((None, None, Q_BLK, K_BAND), lambda h, p, off: (p, h, 0, 0)))
    return pl.pallas_call(
        _bias_tile_kernel,
        grid_spec=grid_spec,
        out_shape=jax.ShapeDtypeStruct((4, NA_HEADS, Q_BLK, K_BAND), F32),
        compiler_params=_params(2),
        name="na_bias_table",
    )(jnp.asarray(off.reshape(-1)), toep)


def na_attention(p, rpb, with_ctx):
    bias = _na_bias_table(rpb)
    lead = 1 if with_ctx else 0

    def lat_block(b, j):
        return b * N_QBLK + j

    def ctx_block(b):
        return N_LAT // Q_BLK + b

    def q_map(b, i):
        return (jnp.where(i < lead, ctx_block(b), lat_block(b, i - lead)), 0)

    def band_map(t, col):
        def f(b, i):
            return (lat_block(b, jnp.clip(i - lead - 1, 0, N_QBLK - 3)) + t, col)
        return f

    def ctx_map(col):
        return lambda b, i: (ctx_block(b), col)

    def bias_map(b, i):
        j = i - lead
        return (jnp.where(j < 0, 3, jnp.where(j == 0, 0, jnp.where(j == N_QBLK - 1, 2, 1))), 0, 0, 0)

    def out_map(b, i):
        return (jnp.where(i < lead, ctx_block(b), lat_block(b, i - lead)), 0)

    blk = (Q_BLK, NA_WIDTH)
    return pl.pallas_call(
        _na_kernel,
        grid=(BATCH, N_QBLK + lead),
        in_specs=[pl.BlockSpec(blk, q_map),
                  pl.BlockSpec(blk, band_map(0, 1)), pl.BlockSpec(blk, band_map(1, 1)),
                  pl.BlockSpec(blk, band_map(2, 1)), pl.BlockSpec(blk, ctx_map(1)),
                  pl.BlockSpec(blk, band_map(0, 2)), pl.BlockSpec(blk, band_map(1, 2)),
                  pl.BlockSpec(blk, band_map(2, 2)), pl.BlockSpec(blk, ctx_map(2)),
                  pl.BlockSpec((None, NA_HEADS, Q_BLK, K_BAND), bias_map, pipeline_mode=pl.Buffered(1))],
        out_specs=pl.BlockSpec(blk, out_map),
        out_shape=jax.ShapeDtypeStruct((N_TOK if with_ctx else N_LAT, NA_WIDTH), BF16),
        compiler_params=_params(2),
        name="na_attention",
    )(p, p, p, p, p, p, p, p, p, bias)


CTX_CHUNKS = CTX_LEN // RET_CHUNK
LAT_CHUNKS = SEQ // RET_CHUNK
RET_STEPS = CTX_CHUNKS + LAT_CHUNKS
RET_Q_COL, RET_K_COL, RET_V_COL, RET_G_COL = 3, 4, 5, 6


def _ret_kernel(ld_ref, q_ref, k_ref, v_ref, cos_ref, sin_ref, *rest, backward):
    if backward:
        of_ref, g_ref, ng_ref, o_ref, st_ref = rest
    else:
        o_ref, st_ref = rest

    @pl.when(pl.program_id(1) == 0)
    def _():
        st_ref[...] = jnp.zeros_like(st_ref)

    ii = lax.broadcasted_iota(jnp.int32, (RET_CHUNK, RET_CHUNK), 0)
    jj = lax.broadcasted_iota(jnp.int32, (RET_CHUNK, RET_CHUNK), 1)
    diff = ((jj - ii) if backward else (ii - jj)).astype(F32)
    causal = diff >= 0.0
    diff = jnp.maximum(diff, 0.0)
    row = lax.broadcasted_iota(jnp.int32, (RET_CHUNK, 1), 0)
    pos = ((RET_CHUNK - 1 - row) if backward else row).astype(F32)
    first_half = (jj % (HEAD_DIM // 2)) < (HEAD_DIM // 4)
    cos = cos_ref[...]
    sin = sin_ref[...]

    def rope(x):
        partner = jnp.where(first_half, pltpu.roll(x, HEAD_DIM - HEAD_DIM // 4, axis=1),
                            pltpu.roll(x, HEAD_DIM // 4, axis=1))
        return x * cos + partner * sin

    for hh in range(RET_HEADS):
        cols = slice(hh * HEAD_DIM, (hh + 1) * HEAD_DIM)
        ld = ld_ref[hh]
        inner_decay = jnp.where(causal, jnp.exp(diff * ld), 0.0)
        q_decay = jnp.exp((pos + 1.0) * ld)
        k_decay = jnp.exp((RET_CHUNK - 1.0 - pos) * ld)
        chunk_decay = jnp.exp(jnp.full((1, HEAD_DIM), RET_CHUNK, F32) * ld)
        q = rope(q_ref[:, cols])
        k = rope(k_ref[:, cols]) * ATT_SCALE
        qb = q.astype(BF16)
        vb = v_ref[:, cols].astype(BF16)
        att = lax.dot_general(qb, k.astype(BF16), (((1,), (1,)), ((), ())), preferred_element_type=F32)
        inner = jnp.dot((att * inner_decay).astype(BF16), vb, preferred_element_type=F32)
        state = st_ref[hh]
        cross = jnp.dot(qb, state.astype(BF16), preferred_element_type=F32) * q_decay
        kd_t = (k * k_decay).T.astype(BF16)
        st_ref[hh] = state * chunk_decay + jnp.dot(kd_t, vb, preferred_element_type=F32)
        o = inner + cross
        if backward:
            tot = of_ref[:, cols] + o
            y = _rms(tot, ng_ref[:, cols]) * _silu(g_ref[:, cols])
            o_ref[:, cols] = y.astype(o_ref.dtype)
        else:
            o_ref[:, cols] = o


def _ret_chunk_block(b, s, backward):
    ctx0 = N_LAT // RET_CHUNK + b * CTX_CHUNKS
    lat0 = b * LAT_CHUNKS
    if backward:
        ctx_blk = ctx0 + (CTX_CHUNKS - 1 - s)
        lat_blk = lat0 + (LAT_CHUNKS - 1 - (s - CTX_CHUNKS))
    else:
        ctx_blk = ctx0 + s
        lat_blk = lat0 + (s - CTX_CHUNKS)
    return jnp.where(s < CTX_CHUNKS, ctx_blk, lat_blk)


def retention_pass(p, log_decay, cos_t, sin_t, backward, o_fwd=None, norm_g=None):
    def col_map(col):
        return lambda b, s: (_ret_chunk_block(b, s, backward), col)

    wide = (RET_CHUNK, RET_WIDTH)
    in_specs = [pl.BlockSpec(memory_space=pltpu.SMEM),
                pl.BlockSpec(wide, col_map(RET_Q_COL)), pl.BlockSpec(wide, col_map(RET_K_COL)),
                pl.BlockSpec(wide, col_map(RET_V_COL)),
                pl.BlockSpec((RET_CHUNK, HEAD_DIM), col_map(0)), pl.BlockSpec((RET_CHUNK, HEAD_DIM), col_map(0))]
    args = [log_decay, p, p, p, cos_t, sin_t]
    if backward:
        in_specs += [pl.BlockSpec(wide, col_map(0)), pl.BlockSpec(wide, col_map(RET_G_COL)),
                     pl.BlockSpec((1, RET_WIDTH), lambda b, s: (0, 0))]
        args += [o_fwd, p, norm_g.reshape(1, RET_WIDTH)]
    return pl.pallas_call(
        functools.partial(_ret_kernel, backward=backward),
        grid=(BATCH, RET_STEPS),
        in_specs=in_specs,
        out_specs=pl.BlockSpec(wide, col_map(0)),
        out_shape=jax.ShapeDtypeStruct((N_TOK, RET_WIDTH), BF16 if backward else F32),
        scratch_shapes=[pltpu.VMEM((RET_HEADS, HEAD_DIM, HEAD_DIM), F32)],
        compiler_params=_params(2),
        name="retention_bwd" if backward else "retention_fwd",
    )(*args)


def _rope_tables():
    quarter = HEAD_DIM // 4
    inv_freq = np.float64(ROPE_BASE) ** (-np.arange(quarter, dtype=np.float64) / quarter)
    t = np.arange(SEQ)
    ang_r = (t // GRID_W)[:, None] * inv_freq[None, :]
    ang_c = (t % GRID_W)[:, None] * inv_freq[None, :]
    cos = np.concatenate([np.cos(ang_r), np.cos(ang_r), np.cos(ang_c), np.cos(ang_c)], axis=-1)
    sin = np.concatenate([-np.sin(ang_r), np.sin(ang_r), -np.sin(ang_c), np.sin(ang_c)], axis=-1)
    cos = np.concatenate([cos] * BATCH + [np.ones((N_CTX, HEAD_DIM))], axis=0).astype(np.float32)
    sin = np.concatenate([sin] * BATCH + [np.zeros((N_CTX, HEAD_DIM))], axis=0).astype(np.float32)
    return jnp.asarray(cos), jnp.asarray(sin)


SGU_TOK = 512
SGU_COLS = 512
SGU_U_COL = (3 * NA_WIDTH + 4 * RET_WIDTH) // SGU_COLS
SGU_V_COL = SGU_U_COL + SGU_WIDTH // SGU_COLS
SGU_GPB = SGU_COLS // HEAD_DIM


def _sgu_kernel(u_ref, v_ref, lng_ref, lnb_ref, w_ref, bs_ref, o_ref):
    for g in range(SGU_GPB):
        cols = slice(g * HEAD_DIM, (g + 1) * HEAD_DIM)
        w = w_ref[g].astype(BF16)
        lng = lng_ref[:, cols]
        lnb = lnb_ref[:, cols]
        bias = bs_ref[:, g:g + 1]
        for c in range(SGU_TOK // SGU_CHUNK):
            rows = slice(c * SGU_CHUNK, (c + 1) * SGU_CHUNK)
            v = _gelu_tanh(v_ref[rows, cols])
            mu = jnp.mean(v, axis=-1, keepdims=True)
            vc = v - mu
            var = jnp.mean(vc * vc, axis=-1, keepdims=True)
            vn = vc * lax.rsqrt(var + NORM_EPS) * lng + lnb
            s = jnp.dot(w, vn.astype(BF16), preferred_element_type=F32) + bias
            o_ref[rows, cols] = (_gelu_tanh(u_ref[rows, cols]) * s).astype(o_ref.dtype)


def spatial_gating(p, ln_g, ln_b, w_s, b_s, n_rows):
    n_tiles = n_rows // SGU_TOK
    n_half = SGU_WIDTH // SGU_COLS
    bs_t = b_s.reshape(n_half, SGU_GPB, SGU_CHUNK).transpose(0, 2, 1)
    return pl.pallas_call(
        _sgu_kernel,
        grid=(n_tiles, n_half),
        in_specs=[pl.BlockSpec((SGU_TOK, SGU_COLS), lambda t, c: (t, SGU_U_COL + c)),
                  pl.BlockSpec((SGU_TOK, SGU_COLS), lambda t, c: (t, SGU_V_COL + c)),
                  pl.BlockSpec((1, SGU_COLS), lambda t, c: (0, c)),
                  pl.BlockSpec((1, SGU_COLS), lambda t, c: (0, c)),
                  pl.BlockSpec((SGU_GPB, SGU_CHUNK, SGU_CHUNK), lambda t, c: (c, 0, 0)),
                  pl.BlockSpec((None, SGU_CHUNK, SGU_GPB), lambda t, c: (c, 0, 0))],
        out_specs=pl.BlockSpec((SGU_TOK, SGU_COLS), lambda t, c: (t, c)),
        out_shape=jax.ShapeDtypeStruct((n_rows, SGU_WIDTH), BF16),
        compiler_params=_params(2),
        name="spatial_gating",
    )(p, p, ln_g.reshape(1, SGU_WIDTH), ln_b.reshape(1, SGU_WIDTH), w_s, bs_t)


ROUTER_LANES = 128


def _norm_router_kernel(x_ref, g_ref, sh_ref, sc_ref, rw_ref, f_ref, aff_ref):
    y = _rms(x_ref[...], g_ref[...])
    f = y * (1.0 + sc_ref[0]) + sh_ref[0]
    fb = f.astype(BF16)
    f_ref[...] = fb
    logits = jnp.dot(fb, rw_ref[...].astype(BF16), preferred_element_type=F32)
    lane = lax.broadcasted_iota(jnp.int32, logits.shape, 1)
    logits = jnp.where(lane < N_EXPERTS, logits, MASK_VALUE)
    e = jnp.exp(logits - jnp.max(logits, axis=1, keepdims=True))
    aff_ref[...] = e / jnp.sum(e, axis=1, keepdims=True)


def norm_router(h, g, mod, shift_k, scale_k, router_w, n_rows):
    n_tiles = n_rows // NORM_TILE
    rw = jnp.pad(router_w, ((0, 0), (0, ROUTER_LANES - N_EXPERTS)))

    def mod_map(k):
        return lambda i: (_segment(i, NORM_TILE) * N_MOD + k, 0, 0)

    return pl.pallas_call(
        _norm_router_kernel,
        grid=(n_tiles,),
        in_specs=[pl.BlockSpec((NORM_TILE, D_MODEL), lambda i: (i, 0)),
                  pl.BlockSpec((1, D_MODEL), lambda i: (0, 0)),
                  pl.BlockSpec((1, 1, D_MODEL), mod_map(shift_k)),
                  pl.BlockSpec((1, 1, D_MODEL), mod_map(scale_k)),
                  pl.BlockSpec((D_MODEL, ROUTER_LANES), lambda i: (0, 0))],
        out_specs=[pl.BlockSpec((NORM_TILE, D_MODEL), lambda i: (i, 0)),
                   pl.BlockSpec((NORM_TILE, ROUTER_LANES), lambda i: (i, 0))],
        out_shape=[jax.ShapeDtypeStruct((n_rows, D_MODEL), BF16),
                   jax.ShapeDtypeStruct((n_rows, ROUTER_LANES), F32)],
        compiler_params=_params(1),
        name="norm_router",
    )(h, g.reshape(1, D_MODEL), mod, mod, rw)


FF_CHUNK = 256
DOWN_TN = 1024


def _ffn_up_kernel(x_ref, wg_ref, wu_ref, o_ref):
    x = x_ref[...]
    gate = jnp.dot(x, wg_ref[...].astype(BF16), preferred_element_type=F32)
    up = jnp.dot(x, wu_ref[...].astype(BF16), preferred_element_type=F32)
    o_ref[...] = (_silu(gate) * up).astype(o_ref.dtype)


def _ffn_down_kernel(h_ref, wd_ref, o_ref):
    o_ref[...] = jnp.dot(h_ref[...], wd_ref[...].astype(BF16), preferred_element_type=F32)


def expert_ffn(xin, w_gate, w_up, w_down, li):
    r = xin.shape[1]
    hid = pl.pallas_call(
        _ffn_up_kernel,
        grid=(N_EXPERTS, EXPERT_FF // FF_CHUNK),
        in_specs=[pl.BlockSpec((None, r, D_MODEL), lambda e, f: (e, 0, 0)),
                  pl.BlockSpec((None, None, D_MODEL, FF_CHUNK), lambda e, f: (li, e, 0, f)),
                  pl.BlockSpec((None, None, D_MODEL, FF_CHUNK), lambda e, f: (li, e, 0, f))],
        out_specs=pl.BlockSpec((None, r, FF_CHUNK), lambda e, f: (e, 0, f)),
        out_shape=jax.ShapeDtypeStruct((N_EXPERTS, r, EXPERT_FF), BF16),
        compiler_params=_params(2),
        name="ffn_up",
    )(xin, w_gate, w_up)
    return pl.pallas_call(
        _ffn_down_kernel,
        grid=(N_EXPERTS, D_MODEL // DOWN_TN),
        in_specs=[pl.BlockSpec((None, r, EXPERT_FF), lambda e, n: (e, 0, 0)),
                  pl.BlockSpec((None, None, EXPERT_FF, DOWN_TN), lambda e, n: (li, e, 0, n))],
        out_specs=pl.BlockSpec((None, r, DOWN_TN), lambda e, n: (e, 0, n)),
        out_shape=jax.ShapeDtypeStruct((N_EXPERTS, r, D_MODEL), F32),
        compiler_params=_params(2),
        name="ffn_down",
    )(hid, w_down)


COMBINE_CHUNK = 256
COMBINE_UNROLL = 8


def _combine_kernel(rows_ref, y_ref, gate_ref, gf_ref, h_in_ref, h_ref, buf_ref, gsem, ssem, *, chunk, n_chunks):
    del h_in_ref
    e = pl.program_id(0)
    c = pl.program_id(1)
    n_e = pl.num_programs(0)
    slot = c % 2

    def copy(hbm, vmem, s, gather):
        return (pltpu.make_async_copy(hbm, vmem, gsem.at[s]) if gather
                else pltpu.make_async_copy(vmem, hbm, ssem.at[s]))

    def start_all(s, ee, cc, gather):
        base = (ee * n_chunks + cc) * chunk

        def body(g, carry):
            for u in range(COMBINE_UNROLL):
                j = g * COMBINE_UNROLL + u
                row = rows_ref[base + j]
                copy(h_ref.at[pl.ds(row, 1)], buf_ref.at[s, pl.ds(j, 1)], s, gather).start(priority=u % 2)
            return carry
        lax.fori_loop(0, chunk // COMBINE_UNROLL, body, 0)

    def wait_all(s, gather):
        copy(h_ref.at[pl.ds(0, chunk)], buf_ref.at[s], s, gather).wait()

    @pl.when((e == 0) & (c == 0))
    def _():
        start_all(0, 0, 0, True)

    wait_all(slot, True)
    buf_ref[slot] = buf_ref[slot] + gf_ref[0] * (gate_ref[...] * y_ref[...])

    @pl.when(c > 0)
    def _():
        wait_all(1 - slot, False)

    @pl.when(c < n_chunks - 1)
    def _():
        start_all(1 - slot, e, c + 1, True)
        start_all(slot, e, c, False)

    @pl.when(c == n_chunks - 1)
    def _():
        start_all(slot, e, c, False)
        wait_all(slot, False)

        @pl.when(e < n_e - 1)
        def _():
            start_all(0, e + 1, 0, True)


def moe_combine(h, y, gates, rows, mod, gate_k, seg_of_chunk, chunk, row_off):
    n_e, r = rows.shape
    n_chunks = r // chunk
    blk0 = row_off // chunk
    d = h.shape[1]
    grid_spec = pltpu.PrefetchScalarGridSpec(
        num_scalar_prefetch=1,
        grid=(n_e, n_chunks),
        in_specs=[pl.BlockSpec((None, chunk, d), lambda e, c, rows: (e, blk0 + c, 0)),
                  pl.BlockSpec((None, chunk, 1), lambda e, c, rows: (e, c, 0)),
                  pl.BlockSpec((1, 1, d), lambda e, c, rows: (seg_of_chunk(c) * N_MOD + gate_k, 0, 0)),
                  pl.BlockSpec(memory_space=pl.ANY)],
        out_specs=pl.BlockSpec(memory_space=pl.ANY),
        scratch_shapes=[pltpu.VMEM((2, chunk, d), F32),
                        pltpu.SemaphoreType.DMA((2,)), pltpu.SemaphoreType.DMA((2,))])
    return pl.pallas_call(
        functools.partial(_combine_kernel, chunk=chunk, n_chunks=n_chunks),
        grid_spec=grid_spec,
        out_shape=jax.ShapeDtypeStruct(h.shape, h.dtype),
        input_output_aliases={4: 0},
        compiler_params=_params(2),
        name="moe_combine",
    )(rows.reshape(-1), y, gates.reshape(n_e, r, 1), mod, h)


def _final_norm_kernel(x_ref, g_ref, o_ref):
    o_ref[...] = _rms(x_ref[...], g_ref[...])


def final_norm(h, g):
    return pl.pallas_call(
        _final_norm_kernel,
        grid=(N_LAT // NORM_TILE,),
        in_specs=[pl.BlockSpec((NORM_TILE, D_MODEL), lambda i: (i, 0)),
                  pl.BlockSpec((1, D_MODEL), lambda i: (0, 0))],
        out_specs=pl.BlockSpec((NORM_TILE, D_MODEL), lambda i: (i, 0)),
        out_shape=jax.ShapeDtypeStruct((N_LAT, D_MODEL), F32),
        compiler_params=_params(1),
        name="final_norm",
    )(h, g.reshape(1, D_MODEL))


def _route(aff, n_per_sample, row0):
    cap = EC_CAPACITY * n_per_sample // N_EXPERTS
    a = aff[row0:row0 + BATCH * n_per_sample, :N_EXPERTS].reshape(BATCH, n_per_sample, N_EXPERTS)
    gates, idx = lax.top_k(a.transpose(0, 2, 1), cap)
    rows = idx + (row0 + jnp.arange(BATCH) * n_per_sample)[:, None, None]
    return rows, gates


def _moe(h, f, aff, mod, big, li, update_ctx):
    def per_expert(t):
        return t.transpose(1, 0, 2).reshape(N_EXPERTS, -1)

    rows, gates = _route(aff, SEQ, 0)
    chunks_per_sample = rows.shape[2] // COMBINE_CHUNK
    parts = [(per_expert(rows), per_expert(gates), lambda c: c // chunks_per_sample, COMBINE_CHUNK)]
    if update_ctx:
        rows_c, gates_c = _route(aff, CTX_LEN, N_LAT)
        parts.append((per_expert(rows_c), per_expert(gates_c), lambda c: MOD_CTX_ROW, BATCH * rows_c.shape[2]))
    all_rows = jnp.concatenate([part[0] for part in parts], axis=1)
    y = expert_ffn(f[all_rows], big['w_gate'], big['w_up'], big['w_down'], li)
    off = 0
    for part_rows, part_gates, seg_of_chunk, chunk in parts:
        h = moe_combine(h, y, part_gates, part_rows, mod, 5, seg_of_chunk, chunk, off)
        off += part_rows.shape[1]
    return h


def _layer(h, c_rows, big, li, lp, cos_t, sin_t, update_ctx):
    n_rows = N_TOK if update_ctx else N_LAT
    mod = adaln(c_rows, big['ada_w'], lp['ada_b'], li)[:3].reshape(3 * N_MOD, 1, D_MODEL)
    if isinstance(h, tuple):
        a, h = norm_mod_first(h[0], h[1], lp['norm1_g'], mod, 0, 1)
    else:
        a = norm_mod(h, lp['norm1_g'], mod, 0, 1)
    p = in_proj(a, big['w_in'], li)
    o_a = na_attention(p, lp['na_rpb'], with_ctx=update_ctx)
    o_f = retention_pass(p, lp['ret_log_decay'][0], cos_t, sin_t, backward=False)
    o_b = retention_pass(p, lp['ret_log_decay'][1], cos_t, sin_t, backward=True, o_fwd=o_f,
                         norm_g=lp['ret_norm_g'])
    o_c = spatial_gating(p, lp['sgu_ln_g'], lp['sgu_ln_b'], lp['sgu_w'], lp['sgu_b'], n_rows)
    h = out_proj_residual(o_a, o_b, o_c, big['w_out'], li, h, mod, 2, n_rows)
    f, aff = norm_router(h, lp['norm2_g'], mod, 3, 4, lp['router_w'], n_rows)
    return _moe(h, f, aff, mod, big, li, update_ctx)


def kernel(x, c, ctx, c_ctx, ada_w, ada_b, norm1_g, norm2_g, w_in, w_out, na_rpb, ret_log_decay, ret_norm_g,
           sgu_ln_g, sgu_ln_b, sgu_w, sgu_b, router_w, w_gate, w_up, w_down, final_g):
    h = (x.reshape(N_LAT, D_MODEL), ctx.reshape(N_CTX, D_MODEL))
    c_rows = jnp.zeros((ADA_ROWS, D_MODEL), F32).at[:BATCH].set(c).at[MOD_CTX_ROW].set(c_ctx)
    cos_t, sin_t = _rope_tables()
    big = {'ada_w': ada_w, 'w_in': w_in, 'w_out': w_out, 'w_gate': w_gate, 'w_up': w_up, 'w_down': w_down}
    for i in range(DEPTH):
        lp = {
            'ada_b': ada_b[i], 'norm1_g': norm1_g[i], 'norm2_g': norm2_g[i],
            'na_rpb': na_rpb[i], 'ret_log_decay': ret_log_decay[i],
            'ret_norm_g': ret_norm_g[i], 'sgu_ln_g': sgu_ln_g[i], 'sgu_ln_b': sgu_ln_b[i],
            'sgu_w': sgu_w[i], 'sgu_b': sgu_b[i], 'router_w': router_w[i],
        }
        h = _layer(h, c_rows, big, i, lp, cos_t, sin_t, update_ctx=(i < DEPTH - 1))
    return final_norm(h, final_g).reshape(BATCH, SEQ, D_MODEL)
```

```python
import functools

import numpy as np
import jax
import jax.numpy as jnp
from jax import lax
from jax.experimental import pallas as pl
from jax.experimental.pallas import tpu as pltpu

D_MODEL = 4096
BATCH = 2
SEQ = 4096
DEPTH = 2
GRID_W = 64
GRID_ROWS = SEQ // GRID_W
CTX_LEN = 256
HEAD_DIM = 128
NA_HEADS = 12
RET_HEADS = 12
SGU_GROUPS = 8
SGU_CHUNK = 128
RET_CHUNK = 128
WIN_ROWS = 8
WIN_COLS = 16
NA_WIDTH = NA_HEADS * HEAD_DIM
RET_WIDTH = RET_HEADS * HEAD_DIM
SGU_WIDTH = SGU_GROUPS * HEAD_DIM
MIX_WIDTH = NA_WIDTH + RET_WIDTH + SGU_WIDTH
IN_WIDTH = 3 * NA_WIDTH + 4 * RET_WIDTH + 2 * SGU_WIDTH
N_EXPERTS = 16
EXPERT_FF = 1024
EC_CAPACITY = 2
N_MOD = 6
ROPE_BASE = 10000.0
NORM_EPS = 1e-6

N_CTX = BATCH * CTX_LEN
N_LAT = BATCH * SEQ
N_TOK = N_CTX + N_LAT
MOD_CTX_ROW = BATCH
ATT_SCALE = HEAD_DIM ** -0.5
MASK_VALUE = -1e30

Q_ROWS = 4
Q_BLK = Q_ROWS * GRID_W
K_BAND_ROWS = 12
K_BAND = K_BAND_ROWS * GRID_W
N_QBLK = GRID_ROWS // Q_ROWS

VMEM_LIMIT = 56 * 1024 * 1024

F32 = jnp.float32
BF16 = jnp.bfloat16


def _params(n_axes, vmem=VMEM_LIMIT):
    return pltpu.CompilerParams(dimension_semantics=("arbitrary",) * n_axes, vmem_limit_bytes=vmem)


def _segment(tile_idx, tile_rows):
    return jnp.where(tile_idx < N_LAT // tile_rows, tile_idx // (SEQ // tile_rows), MOD_CTX_ROW)


def _silu(x):
    return x / (1.0 + jnp.exp(-x))


def _gelu_tanh(x):
    return x * (0.5 * (1.0 + jnp.tanh(np.float32(np.sqrt(2.0 / np.pi)) * (x + 0.044715 * (x * x * x)))))


ADA_TN = 512
ADA_ROWS = 16


def _adaln_kernel(c_ref, w_ref, b_ref, o_ref):
    s = _silu(c_ref[...]).astype(BF16)
    o_ref[...] = jnp.dot(s, w_ref[...].astype(BF16), preferred_element_type=F32) + b_ref[...]


def adaln(c_rows, ada_w, ada_b, li):
    n = ada_w.shape[2]
    return pl.pallas_call(
        _adaln_kernel,
        grid=(n // ADA_TN,),
        in_specs=[pl.BlockSpec((ADA_ROWS, D_MODEL), lambda j: (0, 0)),
                  pl.BlockSpec((None, D_MODEL, ADA_TN), lambda j: (li, 0, j)),
                  pl.BlockSpec((1, ADA_TN), lambda j: (0, j))],
        out_specs=pl.BlockSpec((ADA_ROWS, ADA_TN), lambda j: (0, j)),
        out_shape=jax.ShapeDtypeStruct((ADA_ROWS, n), F32),
        compiler_params=_params(1),
        name="adaln",
    )(c_rows, ada_w, ada_b.reshape(1, n))


NORM_TILE = 256


def _rms(x, g):
    return x * lax.rsqrt(jnp.mean(x * x, axis=-1, keepdims=True) + NORM_EPS) * g


def _norm_mod_kernel(x_ref, g_ref, sh_ref, sc_ref, o_ref):
    y = _rms(x_ref[...], g_ref[...])
    o_ref[...] = (y * (1.0 + sc_ref[0]) + sh_ref[0]).astype(o_ref.dtype)


def norm_mod(h, g, mod, shift_k, scale_k):
    def mod_map(k):
        return lambda i: (_segment(i, NORM_TILE) * N_MOD + k, 0, 0)

    return pl.pallas_call(
        _norm_mod_kernel,
        grid=(N_TOK // NORM_TILE,),
        in_specs=[pl.BlockSpec((NORM_TILE, D_MODEL), lambda i: (i, 0)),
                  pl.BlockSpec((1, D_MODEL), lambda i: (0, 0)),
                  pl.BlockSpec((1, 1, D_MODEL), mod_map(shift_k)),
                  pl.BlockSpec((1, 1, D_MODEL), mod_map(scale_k))],
        out_specs=pl.BlockSpec((NORM_TILE, D_MODEL), lambda i: (i, 0)),
        out_shape=jax.ShapeDtypeStruct((N_TOK, D_MODEL), BF16),
        compiler_params=_params(1),
        name="norm_mod",
    )(h, g.reshape(1, D_MODEL), mod, mod)


def _norm_mod_first_kernel(x_ref, c_ref, g_ref, sh_ref, sc_ref, o_ref, h_ref):
    src = jnp.where(pl.program_id(0) < N_LAT // NORM_TILE, x_ref[...], c_ref[...])
    h_ref[...] = src
    y = _rms(src, g_ref[...])
    o_ref[...] = (y * (1.0 + sc_ref[0]) + sh_ref[0]).astype(o_ref.dtype)


def norm_mod_first(x2d, ctx2d, g, mod, shift_k, scale_k):
    n_lat_tiles = N_LAT // NORM_TILE

    def mod_map(k):
        return lambda i: (_segment(i, NORM_TILE) * N_MOD + k, 0, 0)

    return pl.pallas_call(
        _norm_mod_first_kernel,
        grid=(N_TOK // NORM_TILE,),
        in_specs=[pl.BlockSpec((NORM_TILE, D_MODEL), lambda i: (jnp.minimum(i, n_lat_tiles - 1), 0)),
                  pl.BlockSpec((NORM_TILE, D_MODEL), lambda i: (jnp.maximum(i - n_lat_tiles, 0), 0)),
                  pl.BlockSpec((1, D_MODEL), lambda i: (0, 0)),
                  pl.BlockSpec((1, 1, D_MODEL), mod_map(shift_k)),
                  pl.BlockSpec((1, 1, D_MODEL), mod_map(scale_k))],
        out_specs=[pl.BlockSpec((NORM_TILE, D_MODEL), lambda i: (i, 0)),
                   pl.BlockSpec((NORM_TILE, D_MODEL), lambda i: (i, 0))],
        out_shape=[jax.ShapeDtypeStruct((N_TOK, D_MODEL), BF16),
                   jax.ShapeDtypeStruct((N_TOK, D_MODEL), F32)],
        compiler_params=_params(1),
        name="norm_mod_first",
    )(x2d, ctx2d, g.reshape(1, D_MODEL), mod, mod)


PROJ_TM = 2176
PROJ_TN = 256


def _proj_kernel(a_ref, w_ref, o_ref):
    o_ref[...] = jnp.dot(a_ref[...], w_ref[...].astype(BF16), preferred_element_type=F32)


def in_proj(a, w_in, li):
    return pl.pallas_call(
        _proj_kernel,
        grid=(N_TOK // PROJ_TM, IN_WIDTH // PROJ_TN),
        in_specs=[pl.BlockSpec((PROJ_TM, D_MODEL), lambda m, n: (m, 0), pipeline_mode=pl.Buffered(1)),
                  pl.BlockSpec((None, D_MODEL, PROJ_TN), lambda m, n: (li, 0, n))],
        out_specs=pl.BlockSpec((PROJ_TM, PROJ_TN), lambda m, n: (m, n)),
        out_shape=jax.ShapeDtypeStruct((N_TOK, IN_WIDTH), F32),
        compiler_params=_params(2),
        name="in_proj",
    )(a, w_in)


OUT_TN = 256


def _out_proj_kernel(oa_ref, ob_ref, oc_ref, w_ref, h_ref, g0_ref, g1_ref, gc_ref, o_ref):
    w = w_ref[...].astype(BF16)
    y = jnp.dot(oa_ref[...], w[0:NA_WIDTH], preferred_element_type=F32)
    y = y + jnp.dot(ob_ref[...], w[NA_WIDTH:NA_WIDTH + RET_WIDTH], preferred_element_type=F32)
    y = y + jnp.dot(oc_ref[...], w[NA_WIDTH + RET_WIDTH:MIX_WIDTH], preferred_element_type=F32)
    tm = o_ref.shape[0]
    row = pl.program_id(0) * tm + lax.broadcasted_iota(jnp.int32, (tm, 1), 0)
    gate = jnp.where(row < SEQ, g0_ref[0], jnp.where(row < N_LAT, g1_ref[0], gc_ref[0]))
    o_ref[...] = h_ref[...] + gate * y


def out_proj_residual(o_a, o_b, o_c, w_out, li, h, mod, gate_k, n_rows):
    tm = n_rows // 4

    def gate_map(seg):
        return lambda m, n: (seg * N_MOD + gate_k, 0, n)

    def row_tile(width):
        return pl.BlockSpec((tm, width), lambda m, n: (m, 0), pipeline_mode=pl.Buffered(1))

    gate_spec = [pl.BlockSpec((1, 1, OUT_TN), gate_map(seg)) for seg in (0, 1, MOD_CTX_ROW)]
    return pl.pallas_call(
        _out_proj_kernel,
        grid=(n_rows // tm, D_MODEL // OUT_TN),
        in_specs=[row_tile(NA_WIDTH), row_tile(RET_WIDTH), row_tile(SGU_WIDTH),
                  pl.BlockSpec((None, MIX_WIDTH, OUT_TN), lambda m, n: (li, 0, n)),
                  pl.BlockSpec((tm, OUT_TN), lambda m, n: (m, n))] + gate_spec,
        out_specs=pl.BlockSpec((tm, OUT_TN), lambda m, n: (m, n)),
        out_shape=jax.ShapeDtypeStruct((N_TOK, D_MODEL), F32),
        input_output_aliases={4: 0},
        compiler_params=_params(2),
        name="out_proj",
    )(o_a, o_b, o_c, w_out, h, mod, mod, mod)


def _na_kernel(q_ref, k0_ref, k1_ref, k2_ref, kc_ref, v0_ref, v1_ref, v2_ref, vc_ref, bias_ref, o_ref):
    for h in range(NA_HEADS):
        cols = slice(h * HEAD_DIM, (h + 1) * HEAD_DIM)
        q = (q_ref[:, cols] * ATT_SCALE).astype(BF16)

        def scores(k_ref):
            k = k_ref[:, cols].astype(BF16)
            return lax.dot_general(q, k, (((1,), (1,)), ((), ())), preferred_element_type=F32)

        s_win = jnp.concatenate([scores(k0_ref), scores(k1_ref), scores(k2_ref)], axis=1) + bias_ref[h]
        s_ctx = scores(kc_ref)
        m = jnp.maximum(jnp.max(s_win, axis=1, keepdims=True), jnp.max(s_ctx, axis=1, keepdims=True))
        e_win = jnp.exp(s_win - m)
        e_ctx = jnp.exp(s_ctx - m)
        denom = jnp.sum(e_win, axis=1, keepdims=True) + jnp.sum(e_ctx, axis=1, keepdims=True)
        acc = jnp.dot(e_ctx.astype(BF16), vc_ref[:, cols].astype(BF16), preferred_element_type=F32)
        for t, v_ref in enumerate((v0_ref, v1_ref, v2_ref)):
            e = e_win[:, t * Q_BLK:(t + 1) * Q_BLK].astype(BF16)
            acc = acc + jnp.dot(e, v_ref[:, cols].astype(BF16), preferred_element_type=F32)
        o_ref[:, cols] = (acc / denom).astype(o_ref.dtype)


def _bias_tile_kernel(toep_ref, o_ref, *, off):
    for p in range(off.shape[0]):
        for q in range(Q_ROWS):
            for k in range(0, K_BAND_ROWS, 2):
                pair = jnp.concatenate([toep_ref[int(off[p, q, k])], toep_ref[int(off[p, q, k + 1])]], axis=1)
                o_ref[p, q * GRID_W:(q + 1) * GRID_W, k * GRID_W:(k + 2) * GRID_W] = pair


def _na_bias_table(rpb):
    n_off = 2 * WIN_ROWS - 1
    n_rel = 2 * WIN_COLS - 1
    qc = np.arange(GRID_W)
    rel = qc[None, :] - qc[:, None] + (WIN_COLS - 1)
    c0 = np.clip(qc - WIN_COLS // 2, 0, GRID_W - WIN_COLS)
    ok_c = (qc[None, :] >= c0[:, None]) & (qc[None, :] < c0[:, None] + WIN_COLS)
    onehot = ((rel[None] == np.arange(n_rel)[:, None, None]) & ok_c[None]).astype(np.float32).reshape(n_rel, -1)
    toep = jnp.dot(rpb.astype(F32).reshape(-1, n_rel), onehot, precision=lax.Precision.HIGHEST)
    toep = jnp.where(ok_c.reshape(-1), toep, MASK_VALUE).reshape(NA_HEADS, n_off, GRID_W, GRID_W)
    toep = jnp.pad(toep, ((0, 0), (0, 1), (0, 0), (0, 0)), constant_values=MASK_VALUE)
    off = np.full((4, Q_ROWS, K_BAND_ROWS), n_off, np.int32)
    for pat, blk in enumerate((0, 1, N_QBLK - 1)):
        band0 = int(np.clip(Q_ROWS * blk - Q_ROWS, 0, GRID_ROWS - K_BAND_ROWS))
        for q in range(Q_ROWS):
            qr = Q_ROWS * blk + q
            r0 = int(np.clip(qr - WIN_ROWS // 2, 0, GRID_ROWS - WIN_ROWS))
            for k in range(K_BAND_ROWS):
                kr = band0 + k
                if r0 <= kr < r0 + WIN_ROWS:
                    off[pat, q, k] = kr - qr + (WIN_ROWS - 1)
    return pl.pallas_call(
        functools.partial(_bias_tile_kernel, off=off),
        grid=(NA_HEADS,),
        in_specs=[pl.BlockSpec((None, n_off + 1, GRID_W, GRID_W), lambda h: (h, 0, 0, 0))],
        out_specs=pl.BlockSpec((4, None, Q_BLK, K_BAND), lambda h: (0, h, 0, 0)),
        out_shape=jax.ShapeDtypeStruct((4, NA_HEADS, Q_BLK, K_BAND), F32),
        compiler_params=_params(1),
        name="na_bias_table",
    )(toep)


def na_attention(p, rpb, with_ctx):
    bias = _na_bias_table(rpb)
    lead = 1 if with_ctx else 0

    def lat_block(b, j):
        return b * N_QBLK + j

    def ctx_block(b):
        return N_LAT // Q_BLK + b

    def q_map(b, i):
        return (jnp.where(i < lead, ctx_block(b), lat_block(b, i - lead)), 0)

    def band_map(t, col):
        def f(b, i):
            return (lat_block(b, jnp.clip(i - lead - 1, 0, N_QBLK - 3)) + t, col)
        return f

    def ctx_map(col):
        return lambda b, i: (ctx_block(b), col)

    def bias_map(b, i):
        j = i - lead
        return (jnp.where(j < 0, 3, jnp.where(j == 0, 0, jnp.where(j == N_QBLK - 1, 2, 1))), 0, 0, 0)

    def out_map(b, i):
        return (jnp.where(i < lead, ctx_block(b), lat_block(b, i - lead)), 0)

    blk = (Q_BLK, NA_WIDTH)
    return pl.pallas_call(
        _na_kernel,
        grid=(BATCH, N_QBLK + lead),
        in_specs=[pl.BlockSpec(blk, q_map),
                  pl.BlockSpec(blk, band_map(0, 1)), pl.BlockSpec(blk, band_map(1, 1)),
                  pl.BlockSpec(blk, band_map(2, 1)), pl.BlockSpec(blk, ctx_map(1)),
                  pl.BlockSpec(blk, band_map(0, 2)), pl.BlockSpec(blk, band_map(1, 2)),
                  pl.BlockSpec(blk, band_map(2, 2)), pl.BlockSpec(blk, ctx_map(2)),
                  pl.BlockSpec((None, NA_HEADS, Q_BLK, K_BAND), bias_map, pipeline_mode=pl.Buffered(1))],
        out_specs=pl.BlockSpec(blk, out_map),
        out_shape=jax.ShapeDtypeStruct((N_TOK if with_ctx else N_LAT, NA_WIDTH), BF16),
        compiler_params=_params(2),
        name="na_attention",
    )(p, p, p, p, p, p, p, p, p, bias)


CTX_CHUNKS = CTX_LEN // RET_CHUNK
LAT_CHUNKS = SEQ // RET_CHUNK
RET_STEPS = CTX_CHUNKS + LAT_CHUNKS
RET_Q_COL, RET_K_COL, RET_V_COL, RET_G_COL = 3, 4, 5, 6


def _ret_kernel(ld_ref, q_ref, k_ref, v_ref, cos_ref, sin_ref, *rest, backward):
    if backward:
        of_ref, g_ref, ng_ref, o_ref, st_ref = rest
    else:
        o_ref, st_ref = rest

    @pl.when(pl.program_id(1) == 0)
    def _():
        st_ref[...] = jnp.zeros_like(st_ref)

    ii = lax.broadcasted_iota(jnp.int32, (RET_CHUNK, RET_CHUNK), 0)
    jj = lax.broadcasted_iota(jnp.int32, (RET_CHUNK, RET_CHUNK), 1)
    diff = ((jj - ii) if backward else (ii - jj)).astype(F32)
    causal = diff >= 0.0
    diff = jnp.maximum(diff, 0.0)
    row = lax.broadcasted_iota(jnp.int32, (RET_CHUNK, 1), 0)
    pos = ((RET_CHUNK - 1 - row) if backward else row).astype(F32)
    first_half = (jj % (HEAD_DIM // 2)) < (HEAD_DIM // 4)
    cos = cos_ref[...]
    sin = sin_ref[...]

    def rope(x):
        partner = jnp.where(first_half, pltpu.roll(x, HEAD_DIM - HEAD_DIM // 4, axis=1),
                            pltpu.roll(x, HEAD_DIM // 4, axis=1))
        return x * cos + partner * sin

    for hh in range(RET_HEADS):
        cols = slice(hh * HEAD_DIM, (hh + 1) * HEAD_DIM)
        ld = ld_ref[hh]
        inner_decay = jnp.where(causal, jnp.exp(diff * ld), 0.0)
        q_decay = jnp.exp((pos + 1.0) * ld)
        k_decay = jnp.exp((RET_CHUNK - 1.0 - pos) * ld)
        chunk_decay = jnp.exp(jnp.full((1, HEAD_DIM), RET_CHUNK, F32) * ld)
        q = rope(q_ref[:, cols])
        k = rope(k_ref[:, cols]) * ATT_SCALE
        qb = q.astype(BF16)
        vb = v_ref[:, cols].astype(BF16)
        att = lax.dot_general(qb, k.astype(BF16), (((1,), (1,)), ((), ())), preferred_element_type=F32)
        inner = jnp.dot((att * inner_decay).astype(BF16), vb, preferred_element_type=F32)
        state = st_ref[hh]
        cross = jnp.dot(qb, state.astype(BF16), preferred_element_type=F32) * q_decay
        kd_t = (k * k_decay).T.astype(BF16)
        st_ref[hh] = state * chunk_decay + jnp.dot(kd_t, vb, preferred_element_type=F32)
        o = inner + cross
        if backward:
            tot = of_ref[:, cols] + o
            y = _rms(tot, ng_ref[:, cols]) * _silu(g_ref[:, cols])
            o_ref[:, cols] = y.astype(o_ref.dtype)
        else:
            o_ref[:, cols] = o


def _ret_chunk_block(b, s, backward):
    ctx0 = N_LAT // RET_CHUNK + b * CTX_CHUNKS
    lat0 = b * LAT_CHUNKS
    if backward:
        ctx_blk = ctx0 + (CTX_CHUNKS - 1 - s)
        lat_blk = lat0 + (LAT_CHUNKS - 1 - (s - CTX_CHUNKS))
    else:
        ctx_blk = ctx0 + s
        lat_blk = lat0 + (s - CTX_CHUNKS)
    return jnp.where(s < CTX_CHUNKS, ctx_blk, lat_blk)


def retention_pass(p, log_decay, cos_t, sin_t, backward, o_fwd=None, norm_g=None):
    def col_map(col):
        return lambda b, s: (_ret_chunk_block(b, s, backward), col)

    wide = (RET_CHUNK, RET_WIDTH)
    in_specs = [pl.BlockSpec(memory_space=pltpu.SMEM),
                pl.BlockSpec(wide, col_map(RET_Q_COL)), pl.BlockSpec(wide, col_map(RET_K_COL)),
                pl.BlockSpec(wide, col_map(RET_V_COL)),
                pl.BlockSpec((RET_CHUNK, HEAD_DIM), col_map(0)), pl.BlockSpec((RET_CHUNK, HEAD_DIM), col_map(0))]
    args = [log_decay, p, p, p, cos_t, sin_t]
    if backward:
        in_specs += [pl.BlockSpec(wide, col_map(0)), pl.BlockSpec(wide, col_map(RET_G_COL)),
                     pl.BlockSpec((1, RET_WIDTH), lambda b, s: (0, 0))]
        args += [o_fwd, p, norm_g.reshape(1, RET_WIDTH)]
    return pl.pallas_call(
        functools.partial(_ret_kernel, backward=backward),
        grid=(BATCH, RET_STEPS),
        in_specs=in_specs,
        out_specs=pl.BlockSpec(wide, col_map(0)),
        out_shape=jax.ShapeDtypeStruct((N_TOK, RET_WIDTH), BF16 if backward else F32),
        scratch_shapes=[pltpu.VMEM((RET_HEADS, HEAD_DIM, HEAD_DIM), F32)],
        compiler_params=_params(2),
        name="retention_bwd" if backward else "retention_fwd",
    )(*args)


def _rope_tables():
    quarter = HEAD_DIM // 4
    inv_freq = np.float64(ROPE_BASE) ** (-np.arange(quarter, dtype=np.float64) / quarter)
    t = np.arange(SEQ)
    ang_r = (t // GRID_W)[:, None] * inv_freq[None, :]
    ang_c = (t % GRID_W)[:, None] * inv_freq[None, :]
    cos = np.concatenate([np.cos(ang_r), np.cos(ang_r), np.cos(ang_c), np.cos(ang_c)], axis=-1)
    sin = np.concatenate([-np.sin(ang_r), np.sin(ang_r), -np.sin(ang_c), np.sin(ang_c)], axis=-1)
    cos = np.concatenate([cos] * BATCH + [np.ones((N_CTX, HEAD_DIM))], axis=0).astype(np.float32)
    sin = np.concatenate([sin] * BATCH + [np.zeros((N_CTX, HEAD_DIM))], axis=0).astype(np.float32)
    return jnp.asarray(cos), jnp.asarray(sin)


SGU_TOK = 512
SGU_COLS = 512
SGU_U_COL = (3 * NA_WIDTH + 4 * RET_WIDTH) // SGU_COLS
SGU_V_COL = SGU_U_COL + SGU_WIDTH // SGU_COLS
SGU_GPB = SGU_COLS // HEAD_DIM


def _sgu_kernel(u_ref, v_ref, lng_ref, lnb_ref, w_ref, bs_ref, o_ref):
    for g in range(SGU_GPB):
        cols = slice(g * HEAD_DIM, (g + 1) * HEAD_DIM)
        w = w_ref[g].astype(BF16)
        lng = lng_ref[:, cols]
        lnb = lnb_ref[:, cols]
        bias = bs_ref[:, g:g + 1]
        for c in range(SGU_TOK // SGU_CHUNK):
            rows = slice(c * SGU_CHUNK, (c + 1) * SGU_CHUNK)
            v = _gelu_tanh(v_ref[rows, cols])
            mu = jnp.mean(v, axis=-1, keepdims=True)
            vc = v - mu
            var = jnp.mean(vc * vc, axis=-1, keepdims=True)
            vn = vc * lax.rsqrt(var + NORM_EPS) * lng + lnb
            s = jnp.dot(w, vn.astype(BF16), preferred_element_type=F32) + bias
            o_ref[rows, cols] = (_gelu_tanh(u_ref[rows, cols]) * s).astype(o_ref.dtype)


def spatial_gating(p, ln_g, ln_b, w_s, b_s, n_rows):
    n_tiles = n_rows // SGU_TOK
    n_half = SGU_WIDTH // SGU_COLS
    bs_t = b_s.reshape(n_half, SGU_GPB, SGU_CHUNK).transpose(0, 2, 1)
    return pl.pallas_call(
        _sgu_kernel,
        grid=(n_tiles, n_half),
        in_specs=[pl.BlockSpec((SGU_TOK, SGU_COLS), lambda t, c: (t, SGU_U_COL + c)),
                  pl.BlockSpec((SGU_TOK, SGU_COLS), lambda t, c: (t, SGU_V_COL + c)),
                  pl.BlockSpec((1, SGU_COLS), lambda t, c: (0, c)),
                  pl.BlockSpec((1, SGU_COLS), lambda t, c: (0, c)),
                  pl.BlockSpec((SGU_GPB, SGU_CHUNK, SGU_CHUNK), lambda t, c: (c, 0, 0)),
                  pl.BlockSpec((None, SGU_CHUNK, SGU_GPB), lambda t, c: (c, 0, 0))],
        out_specs=pl.BlockSpec((SGU_TOK, SGU_COLS), lambda t, c: (t, c)),
        out_shape=jax.ShapeDtypeStruct((n_rows, SGU_WIDTH), BF16),
        compiler_params=_params(2),
        name="spatial_gating",
    )(p, p, ln_g.reshape(1, SGU_WIDTH), ln_b.reshape(1, SGU_WIDTH), w_s, bs_t)


ROUTER_LANES = 128


def _norm_router_kernel(x_ref, g_ref, sh_ref, sc_ref, rw_ref, f_ref, aff_ref):
    y = _rms(x_ref[...], g_ref[...])
    f = y * (1.0 + sc_ref[0]) + sh_ref[0]
    fb = f.astype(BF16)
    f_ref[...] = fb
    logits = jnp.dot(fb, rw_ref[...].astype(BF16), preferred_element_type=F32)
    lane = lax.broadcasted_iota(jnp.int32, logits.shape, 1)
    logits = jnp.where(lane < N_EXPERTS, logits, MASK_VALUE)
    e = jnp.exp(logits - jnp.max(logits, axis=1, keepdims=True))
    aff_ref[...] = e / jnp.sum(e, axis=1, keepdims=True)


def norm_router(h, g, mod, shift_k, scale_k, router_w, n_rows):
    n_tiles = n_rows // NORM_TILE
    rw = jnp.pad(router_w, ((0, 0), (0, ROUTER_LANES - N_EXPERTS)))

    def mod_map(k):
        return lambda i: (_segment(i, NORM_TILE) * N_MOD + k, 0, 0)

    return pl.pallas_call(
        _norm_router_kernel,
        grid=(n_tiles,),
        in_specs=[pl.BlockSpec((NORM_TILE, D_MODEL), lambda i: (i, 0)),
                  pl.BlockSpec((1, D_MODEL), lambda i: (0, 0)),
                  pl.BlockSpec((1, 1, D_MODEL), mod_map(shift_k)),
                  pl.BlockSpec((1, 1, D_MODEL), mod_map(scale_k)),
                  pl.BlockSpec((D_MODEL, ROUTER_LANES), lambda i: (0, 0))],
        out_specs=[pl.BlockSpec((NORM_TILE, D_MODEL), lambda i: (i, 0)),
                   pl.BlockSpec((NORM_TILE, ROUTER_LANES), lambda i: (i, 0))],
        out_shape=[jax.ShapeDtypeStruct((n_rows, D_MODEL), BF16),
                   jax.ShapeDtypeStruct((n_rows, ROUTER_LANES), F32)],
        compiler_params=_params(1),
        name="norm_router",
    )(h, g.reshape(1, D_MODEL), mod, mod, rw)


FF_CHUNK = 256
DOWN_TN = 1024


def _ffn_up_kernel(x_ref, wg_ref, wu_ref, o_ref):
    x = x_ref[...]
    gate = jnp.dot(x, wg_ref[...].astype(BF16), preferred_element_type=F32)
    up = jnp.dot(x, wu_ref[...].astype(BF16), preferred_element_type=F32)
    o_ref[...] = (_silu(gate) * up).astype(o_ref.dtype)


def _ffn_down_kernel(h_ref, wd_ref, o_ref):
    o_ref[...] = jnp.dot(h_ref[...], wd_ref[...].astype(BF16), preferred_element_type=F32)


def expert_ffn(xin, w_gate, w_up, w_down, li):
    r = xin.shape[1]
    hid = pl.pallas_call(
        _ffn_up_kernel,
        grid=(N_EXPERTS, EXPERT_FF // FF_CHUNK),
        in_specs=[pl.BlockSpec((None, r, D_MODEL), lambda e, f: (e, 0, 0)),
                  pl.BlockSpec((None, None, D_MODEL, FF_CHUNK), lambda e, f: (li, e, 0, f)),
                  pl.BlockSpec((None, None, D_MODEL, FF_CHUNK), lambda e, f: (li, e, 0, f))],
        out_specs=pl.BlockSpec((None, r, FF_CHUNK), lambda e, f: (e, 0, f)),
        out_shape=jax.ShapeDtypeStruct((N_EXPERTS, r, EXPERT_FF), BF16),
        compiler_params=_params(2),
        name="ffn_up",
    )(xin, w_gate, w_up)
    return pl.pallas_call(
        _ffn_down_kernel,
        grid=(N_EXPERTS, D_MODEL // DOWN_TN),
        in_specs=[pl.BlockSpec((None, r, EXPERT_FF), lambda e, n: (e, 0, 0)),
                  pl.BlockSpec((None, None, EXPERT_FF, DOWN_TN), lambda e, n: (li, e, 0, n))],
        out_specs=pl.BlockSpec((None, r, DOWN_TN), lambda e, n: (e, 0, n)),
        out_shape=jax.ShapeDtypeStruct((N_EXPERTS, r, D_MODEL), F32),
        compiler_params=_params(2),
        name="ffn_down",
    )(hid, w_down)


COMBINE_CHUNK = 256
COMBINE_UNROLL = 8


def _combine_kernel(rows_ref, y_ref, gate_ref, gf_ref, h_in_ref, h_ref, buf_ref, gsem, ssem, *, chunk, n_chunks):
    del h_in_ref
    e = pl.program_id(0)
    c = pl.program_id(1)
    n_e = pl.num_programs(0)
    slot = c % 2

    def copy(hbm, vmem, s, gather):
        return (pltpu.make_async_copy(hbm, vmem, gsem.at[s]) if gather
                else pltpu.make_async_copy(vmem, hbm, ssem.at[s]))

    def start_all(s, ee, cc, gather):
        base = (ee * n_chunks + cc) * chunk

        def body(g, carry):
            for u in range(COMBINE_UNROLL):
                j = g * COMBINE_UNROLL + u
                row = rows_ref[base + j]
                copy(h_ref.at[pl.ds(row, 1)], buf_ref.at[s, pl.ds(j, 1)], s, gather).start(priority=u % 2)
            return carry
        lax.fori_loop(0, chunk // COMBINE_UNROLL, body, 0)

    def wait_all(s, gather):
        copy(h_ref.at[pl.ds(0, chunk)], buf_ref.at[s], s, gather).wait()

    @pl.when((e == 0) & (c == 0))
    def _():
        start_all(0, 0, 0, True)

    wait_all(slot, True)
    buf_ref[slot] = buf_ref[slot] + gf_ref[0] * (gate_ref[...] * y_ref[...])

    @pl.when(c > 0)
    def _():
        wait_all(1 - slot, False)

    @pl.when(c < n_chunks - 1)
    def _():
        start_all(1 - slot, e, c + 1, True)
        start_all(slot, e, c, False)

    @pl.when(c == n_chunks - 1)
    def _():
        start_all(slot, e, c, False)
        wait_all(slot, False)

        @pl.when(e < n_e - 1)
        def _():
            start_all(0, e + 1, 0, True)


def moe_combine(h, y, gates, rows, mod, gate_k, seg_of_chunk, chunk, row_off):
    n_e, r = rows.shape
    n_chunks = r // chunk
    blk0 = row_off // chunk
    d = h.shape[1]
    grid_spec = pltpu.PrefetchScalarGridSpec(
        num_scalar_prefetch=1,
        grid=(n_e, n_chunks),
        in_specs=[pl.BlockSpec((None, chunk, d), lambda e, c, rows: (e, blk0 + c, 0)),
                  pl.BlockSpec((None, chunk, 1), lambda e, c, rows: (e, c, 0)),
                  pl.BlockSpec((1, 1, d), lambda e, c, rows: (seg_of_chunk(c) * N_MOD + gate_k, 0, 0)),
                  pl.BlockSpec(memory_space=pl.ANY)],
        out_specs=pl.BlockSpec(memory_space=pl.ANY),
        scratch_shapes=[pltpu.VMEM((2, chunk, d), F32),
                        pltpu.SemaphoreType.DMA((2,)), pltpu.SemaphoreType.DMA((2,))])
    return pl.pallas_call(
        functools.partial(_combine_kernel, chunk=chunk, n_chunks=n_chunks),
        grid_spec=grid_spec,
        out_shape=jax.ShapeDtypeStruct(h.shape, h.dtype),
        input_output_aliases={4: 0},
        compiler_params=_params(2),
        name="moe_combine",
    )(rows.reshape(-1), y, gates.reshape(n_e, r, 1), mod, h)


def _final_norm_kernel(x_ref, g_ref, o_ref):
    o_ref[...] = _rms(x_ref[...], g_ref[...])


def final_norm(h, g):
    return pl.pallas_call(
        _final_norm_kernel,
        grid=(N_LAT // NORM_TILE,),
        in_specs=[pl.BlockSpec((NORM_TILE, D_MODEL), lambda i: (i, 0)),
                  pl.BlockSpec((1, D_MODEL), lambda i: (0, 0))],
        out_specs=pl.BlockSpec((NORM_TILE, D_MODEL), lambda i: (i, 0)),
        out_shape=jax.ShapeDtypeStruct((N_LAT, D_MODEL), F32),
        compiler_params=_params(1),
        name="final_norm",
    )(h, g.reshape(1, D_MODEL))


def _route(aff, n_per_sample, row0):
    cap = EC_CAPACITY * n_per_sample // N_EXPERTS
    a = aff[row0:row0 + BATCH * n_per_sample, :N_EXPERTS].reshape(BATCH, n_per_sample, N_EXPERTS)
    gates, idx = lax.top_k(a.transpose(0, 2, 1), cap)
    rows = idx + (row0 + jnp.arange(BATCH) * n_per_sample)[:, None, None]
    return rows, gates


def _moe(h, f, aff, mod, big, li, update_ctx):
    def per_expert(t):
        return t.transpose(1, 0, 2).reshape(N_EXPERTS, -1)

    rows, gates = _route(aff, SEQ, 0)
    chunks_per_sample = rows.shape[2] // COMBINE_CHUNK
    parts = [(per_expert(rows), per_expert(gates), lambda c: c // chunks_per_sample, COMBINE_CHUNK)]
    if update_ctx:
        rows_c, gates_c = _route(aff, CTX_LEN, N_LAT)
        parts.append((per_expert(rows_c), per_expert(gates_c), lambda c: MOD_CTX_ROW, BATCH * rows_c.shape[2]))
    all_rows = jnp.concatenate([part[0] for part in parts], axis=1)
    y = expert_ffn(f[all_rows], big['w_gate'], big['w_up'], big['w_down'], li)
    off = 0
    for part_rows, part_gates, seg_of_chunk, chunk in parts:
        h = moe_combine(h, y, part_gates, part_rows, mod, 5, seg_of_chunk, chunk, off)
        off += part_rows.shape[1]
    return h


def _layer(h, c_rows, big, li, lp, cos_t, sin_t, update_ctx):
    n_rows = N_TOK if update_ctx else N_LAT
    mod = adaln(c_rows, big['ada_w'], lp['ada_b'], li)[:3].reshape(3 * N_MOD, 1, D_MODEL)
    if isinstance(h, tuple):
        a, h = norm_mod_first(h[0], h[1], lp['norm1_g'], mod, 0, 1)
    else:
        a = norm_mod(h, lp['norm1_g'], mod, 0, 1)
    p = in_proj(a, big['w_in'], li)
    o_a = na_attention(p, lp['na_rpb'], with_ctx=update_ctx)
    o_f = retention_pass(p, lp['ret_log_decay'][0], cos_t, sin_t, backward=False)
    o_b = retention_pass(p, lp['ret_log_decay'][1], cos_t, sin_t, backward=True, o_fwd=o_f,
                         norm_g=lp['ret_norm_g'])
    o_c = spatial_gating(p, lp['sgu_ln_g'], lp['sgu_ln_b'], lp['sgu_w'], lp['sgu_b'], n_rows)
    h = out_proj_residual(o_a, o_b, o_c, big['w_out'], li, h, mod, 2, n_rows)
    f, aff = norm_router(h, lp['norm2_g'], mod, 3, 4, lp['router_w'], n_rows)
    return _moe(h, f, aff, mod, big, li, update_ctx)


def kernel(x, c, ctx, c_ctx, ada_w, ada_b, norm1_g, norm2_g, w_in, w_out, na_rpb, ret_log_decay, ret_norm_g,
           sgu_ln_g, sgu_ln_b, sgu_w, sgu_b, router_w, w_gate, w_up, w_down, final_g):
    h = (x.reshape(N_LAT, D_MODEL), ctx.reshape(N_CTX, D_MODEL))
    c_rows = jnp.zeros((ADA_ROWS, D_MODEL), F32).at[:BATCH].set(c).at[MOD_CTX_ROW].set(c_ctx)
    cos_t, sin_t = _rope_tables()
    big = {'ada_w': ada_w, 'w_in': w_in, 'w_out': w_out, 'w_gate': w_gate, 'w_up': w_up, 'w_down': w_down}
    for i in range(DEPTH):
        lp = {
            'ada_b': ada_b[i], 'norm1_g': norm1_g[i], 'norm2_g': norm2_g[i],
            'na_rpb': na_rpb[i], 'ret_log_decay': ret_log_decay[i],
            'ret_norm_g': ret_norm_g[i], 'sgu_ln_g': sgu_ln_g[i], 'sgu_ln_b': sgu_ln_b[i],
            'sgu_w': sgu_w[i], 'sgu_b': sgu_b[i], 'router_w': router_w[i],
        }
        h = _layer(h, c_rows, big, i, lp, cos_t, sin_t, update_ctx=(i < DEPTH - 1))
    return final_norm(h, final_g).reshape(BATCH, SEQ, D_MODEL)
```
